```python
import math
import jax, jax.numpy as jnp
from jax import lax
import numpy as np

D_MODEL = 1024
BATCH = 4
SEQ = 4096
DEPTH = 1
DEC_BATCH = 32
DEC_SEQ = 4
PAST_LEN = 8192
PAGE_SIZE = 128

N_HEADS = 16
HEAD_DIM = 64
N_KV_HEADS = 4
Q_PER_KV = N_HEADS // N_KV_HEADS
ATT_WIDTH = N_HEADS * HEAD_DIM
KV_WIDTH = N_KV_HEADS * HEAD_DIM
MOBA_BLOCK = 256
MOBA_TOPK = 3
Q_CHUNK = 16
ROPE_THETA = 10000.0
ATT_SCALE = 1.0 / math.sqrt(HEAD_DIM)
SSM_EXPAND = 2
D_INNER = SSM_EXPAND * D_MODEL
SSM_HEAD_DIM = 64
SSM_HEADS = D_INNER // SSM_HEAD_DIM
SSM_GROUPS = 4
D_STATE = 128
CONV_WIDTH = 4
CONV_CH = D_INNER + 2 * SSM_GROUPS * D_STATE
SSD_CHUNK = 128
N_EXPERT_GROUPS = 4
EXPERTS_PER_GROUP = 4
N_EXPERTS = N_EXPERT_GROUPS * EXPERTS_PER_GROUP
TOP_K_IN_GROUP = 2
D_EXPERT = D_MODEL // 4
RMS_EPS = 1e-6
IN_SPLITS = (ATT_WIDTH, KV_WIDTH, KV_WIDTH, D_INNER, CONV_CH, SSM_HEADS, D_MODEL, D_MODEL)
IN_COLS = sum(IN_SPLITS)

kernel_name = "hybrid_moba_ssd_hmoe_decode_step"


def rmsnorm(x, w):
    xf = x.astype(jnp.float32)
    y = xf * lax.rsqrt(jnp.mean(xf * xf, axis=-1, keepdims=True) + RMS_EPS)
    return y.astype(x.dtype) * w


def rope(x, pos):
    inv_freq = 1.0 / (ROPE_THETA ** (jnp.arange(0, HEAD_DIM, 2, dtype=jnp.float32) / HEAD_DIM))
    ang = pos.astype(jnp.float32)[:, None] * inv_freq[None, :]
    ang = jnp.concatenate([ang, ang], axis=-1)[:, None, :]
    x1, x2 = jnp.split(x, 2, axis=-1)
    rot = jnp.concatenate([-x2, x1], axis=-1)
    return (x * jnp.cos(ang) + rot * jnp.sin(ang)).astype(x.dtype)


def moba_chunk(q, qpos, k_blk, v_blk, k_mean):
    b = q.shape[0]
    nb = k_blk.shape[2]
    own = qpos // MOBA_BLOCK
    gate = jnp.einsum('bqgrd,bngd->bqgrn', q, k_mean.astype(q.dtype), preferred_element_type=jnp.float32)
    past = jnp.arange(nb)[None, :] < own[:, None]
    gate = jnp.where(past[None, :, None, None, :], gate, -jnp.inf)
    k_sel = min(MOBA_TOPK, nb)
    _, sel = lax.top_k(gate, k_sel)
    own_b = jnp.broadcast_to(own[None, :, None, None, None], sel.shape[:-1] + (1,)).astype(sel.dtype)
    blocks = jnp.concatenate([sel, own_b], axis=-1)
    n_j = k_sel + 1
    is_sel = jnp.arange(n_j) < k_sel
    blk_ok = jnp.where(is_sel, blocks < own[None, :, None, None, None], True)
    kpos = blocks[..., None] * MOBA_BLOCK + jnp.arange(MOBA_BLOCK)
    mask = blk_ok[..., None] & (kpos <= qpos[None, :, None, None, None, None])
    b_ix = jnp.arange(b)[:, None, None, None, None]
    g_ix = jnp.arange(N_KV_HEADS)[None, None, :, None, None]
    kg = k_blk[b_ix, g_ix, blocks]
    vg = v_blk[b_ix, g_ix, blocks]
    s = jnp.einsum('bqgrd,bqgrjkd->bqgrjk', q, kg, preferred_element_type=jnp.float32) * ATT_SCALE
    s = jnp.where(mask, s, -jnp.inf)
    shp = s.shape
    p = jax.nn.softmax(s.reshape(shp[:-2] + (n_j * MOBA_BLOCK,)), axis=-1).reshape(shp)
    return jnp.einsum('bqgrjk,bqgrjkd->bqgrd', p.astype(vg.dtype), vg)


def moba_attention(q, k, v, qpos):
    b, sq = q.shape[0], q.shape[1]
    L = k.shape[1]
    nb = -(-L // MOBA_BLOCK)
    pad = nb * MOBA_BLOCK - L
    kp = jnp.pad(k, ((0, 0), (0, pad), (0, 0), (0, 0)))
    vp = jnp.pad(v, ((0, 0), (0, pad), (0, 0), (0, 0)))
    kb = kp.reshape(b, nb, MOBA_BLOCK, N_KV_HEADS, HEAD_DIM)
    vb = vp.reshape(b, nb, MOBA_BLOCK, N_KV_HEADS, HEAD_DIM)
    k_mean = jnp.mean(kb.astype(jnp.float32), axis=2)
    k_blk = kb.transpose(0, 3, 1, 2, 4)
    v_blk = vb.transpose(0, 3, 1, 2, 4)
    qg = q.reshape(b, sq, N_KV_HEADS, Q_PER_KV, HEAD_DIM)
    if sq > Q_CHUNK and sq % Q_CHUNK == 0:
        n = sq // Q_CHUNK
        qc = qg.reshape(b, n, Q_CHUNK, N_KV_HEADS, Q_PER_KV, HEAD_DIM).transpose(1, 0, 2, 3, 4, 5)
        pc = qpos.reshape(n, Q_CHUNK)
        out = lax.map(lambda a: moba_chunk(a[0], a[1], k_blk, v_blk, k_mean), (qc, pc))
        out = out.transpose(1, 0, 2, 3, 4, 5)
    else:
        out = moba_chunk(qg, qpos, k_blk, v_blk, k_mean)
    return out.reshape(b, sq, ATT_WIDTH)


def causal_dwconv(xbc, prefix, w, bias):
    xp = jnp.concatenate([prefix, xbc], axis=1)
    out = lax.conv_general_dilated(xp, w[:, None, :], window_strides=(1,), padding='VALID',
                                   dimension_numbers=('NWC', 'WIO', 'NWC'),
                                   feature_group_count=xbc.shape[-1])
    return jax.nn.silu(out + bias), xp[:, -(CONV_WIDTH - 1):]


def ssd_scan(x, dt, A, Bm, Cm, h0):
    b, l, h, p = x.shape
    g, n = Bm.shape[2], Bm.shape[3]
    r = h // g
    T = SSD_CHUNK if l % SSD_CHUNK == 0 else l
    c = l // T
    xdt = (x.astype(jnp.float32) * dt[..., None]).reshape(b, c, T, g, r, p)
    Acum = jnp.cumsum((dt * A).reshape(b, c, T, g, r), axis=2)
    Bc = Bm.astype(jnp.float32).reshape(b, c, T, g, n)
    Cc = Cm.astype(jnp.float32).reshape(b, c, T, g, n)
    causal = jnp.tril(jnp.ones((T, T), dtype=bool))[:, :, None, None]
    seg = Acum[:, :, :, None] - Acum[:, :, None, :]
    Lmat = jnp.exp(jnp.where(causal, seg, -jnp.inf))
    CB = jnp.einsum('bctgn,bcsgn->bctsg', Cc, Bc)
    y_diag = jnp.einsum('bctsg,bctsgr,bcsgrp->bctgrp', CB, Lmat, xdt)
    decay_s = jnp.exp(Acum[:, :, -1:] - Acum)
    st = jnp.einsum('bcsgn,bcsgr,bcsgrp->bcgrpn', Bc, decay_s, xdt)
    chunk_decay = jnp.exp(Acum[:, :, -1])

    def step(hc, inp):
        dec, s = inp
        return hc * dec[..., None, None] + s, hc

    h_fin, h_ins = lax.scan(step, h0.astype(jnp.float32).reshape(b, g, r, p, n),
                            (jnp.moveaxis(chunk_decay, 1, 0), jnp.moveaxis(st, 1, 0)))
    h_ins = jnp.moveaxis(h_ins, 0, 1)
    y_off = jnp.einsum('bctgn,bcgrpn,bctgr->bctgrp', Cc, h_ins, jnp.exp(Acum))
    y = (y_diag + y_off).reshape(b, l, h, p)
    return y, h_fin.reshape(b, h, p, n)


def hier_moe(x, w_rg, w_re, w_g, w_u, w_d):
    shp = x.shape
    t = x.reshape(-1, D_MODEL)
    n_tok = t.shape[0]
    lg = jnp.matmul(t, w_rg).astype(jnp.float32)
    pg = jax.nn.softmax(lg, axis=-1)
    gsel = jnp.argmax(lg, axis=-1)
    p_grp = jnp.take_along_axis(pg, gsel[:, None], axis=1)
    le = jnp.matmul(t, w_re).astype(jnp.float32).reshape(n_tok, N_EXPERT_GROUPS, EXPERTS_PER_GROUP)
    le_g = jnp.take_along_axis(le, gsel[:, None, None], axis=1)[:, 0]
    top_v, top_i = lax.top_k(le_g, TOP_K_IN_GROUP)
    w2 = jax.nn.softmax(top_v, axis=-1) * p_grp
    eid = gsel[:, None] * EXPERTS_PER_GROUP + top_i
    gate = jnp.sum(jax.nn.one_hot(eid, N_EXPERTS, dtype=jnp.float32) * w2[..., None], axis=1)
    hid = jax.nn.silu(jnp.einsum('td,edf->tef', t, w_g)) * jnp.einsum('td,edf->tef', t, w_u)
    hid = (hid * gate[:, :, None]).astype(t.dtype)
    out = jnp.einsum('tef,efd->td', hid, w_d)
    return out.reshape(shp).astype(x.dtype)


def decoder_layer(x, pos, k_past, v_past, h0, conv_prefix, lw):
    (norm1_w, w_in, conv_w, conv_b, dt_bias, a_log, d_skip, ssm_norm_w,
     w_pa, w_pb, w_o, norm2_w, w_rg, w_re, w_g, w_u, w_d) = lw
    b, s = x.shape[0], x.shape[1]
    h = rmsnorm(x, norm1_w)
    proj = jnp.matmul(h, w_in)
    q, k, v, z, xbc, dt_raw, ga, gb = jnp.split(proj, list(np.cumsum(IN_SPLITS)[:-1]), axis=-1)
    q = rope(q.reshape(b, s, N_HEADS, HEAD_DIM), pos)
    k = rope(k.reshape(b, s, N_KV_HEADS, HEAD_DIM), pos)
    v = v.reshape(b, s, N_KV_HEADS, HEAD_DIM)
    if k_past is None:
        k_all, v_all = k, v
    else:
        k_all = jnp.concatenate([k_past, k], axis=1)
        v_all = jnp.concatenate([v_past, v], axis=1)
    att = moba_attention(q, k_all, v_all, pos)
    xbc_c, conv_new = causal_dwconv(xbc, conv_prefix, conv_w, conv_b)
    xs, Bm, Cm = jnp.split(xbc_c, [D_INNER, D_INNER + SSM_GROUPS * D_STATE], axis=-1)
    xs = xs.reshape(b, s, SSM_HEADS, SSM_HEAD_DIM)
    Bm = Bm.reshape(b, s, SSM_GROUPS, D_STATE)
    Cm = Cm.reshape(b, s, SSM_GROUPS, D_STATE)
    dt = jax.nn.softplus(dt_raw.astype(jnp.float32) + dt_bias.astype(jnp.float32))
    A = -jnp.exp(a_log.astype(jnp.float32))
    y, h_new = ssd_scan(xs, dt, A, Bm, Cm, h0)
    y = y + d_skip.astype(jnp.float32)[None, None, :, None] * xs.astype(jnp.float32)
    yz = y.reshape(b, s, D_INNER) * jax.nn.silu(z.astype(jnp.float32))
    yz = yz.reshape(b, s, SSM_GROUPS, D_INNER // SSM_GROUPS)
    yz = yz * lax.rsqrt(jnp.mean(yz * yz, axis=-1, keepdims=True) + RMS_EPS)
    ssd_out = yz.reshape(b, s, D_INNER).astype(x.dtype) * ssm_norm_w
    m = jax.nn.sigmoid(ga) * jnp.matmul(att, w_pa) + jax.nn.sigmoid(gb) * jnp.matmul(ssd_out, w_pb)
    x = x + jnp.matmul(m, w_o)
    x = x + hier_moe(rmsnorm(x, norm2_w), w_rg, w_re, w_g, w_u, w_d)
    return x, k, v, h_new.astype(h0.dtype), conv_new


def setup_inputs(seed: int = 0) -> dict:
    key = jax.random.key(seed)
    ks = jax.random.split(key, 32)
    n_pages = PAST_LEN // PAGE_SIZE
    n_phys = (DEC_BATCH * n_pages * 5) // 4
    f32 = jnp.float32
    nrm = lambda k, shape, scale: jax.random.normal(k, shape, f32) * scale
    dt0 = jnp.exp(jax.random.uniform(ks[10], (DEPTH, SSM_HEADS), f32) * (math.log(0.1) - math.log(0.001)) + math.log(0.001))
    page_table = jax.random.permutation(ks[6], n_phys)[: DEC_BATCH * n_pages].reshape(DEC_BATCH, n_pages).astype(jnp.int32)
    return {
        "x_prompt": nrm(ks[0], (BATCH, SEQ, D_MODEL), 1.0),
        "x_sample": nrm(ks[1], (DEC_BATCH, DEC_SEQ, D_MODEL), 1.0),
        "cache_k": nrm(ks[2], (DEPTH, n_phys, PAGE_SIZE, N_KV_HEADS, HEAD_DIM), 1.0),
        "cache_v": nrm(ks[3], (DEPTH, n_phys, PAGE_SIZE, N_KV_HEADS, HEAD_DIM), 1.0),
        "state_ssm": nrm(ks[4], (DEPTH, DEC_BATCH, SSM_HEADS, SSM_HEAD_DIM, D_STATE), 0.1),
        "state_conv": nrm(ks[5], (DEPTH, DEC_BATCH, CONV_WIDTH - 1, CONV_CH), 1.0),
        "page_table": page_table,
        "norm1_w": 1.0 + nrm(ks[7], (DEPTH, D_MODEL), 0.02),
        "w_in": nrm(ks[8], (DEPTH, D_MODEL, IN_COLS), D_MODEL ** -0.5),
        "conv_w": nrm(ks[9], (DEPTH, CONV_WIDTH, CONV_CH), CONV_WIDTH ** -0.5),
        "conv_b": nrm(ks[11], (DEPTH, CONV_CH), 0.02),
        "dt_bias": dt0 + jnp.log(-jnp.expm1(-dt0)),
        "a_log": jnp.log(jax.random.uniform(ks[12], (DEPTH, SSM_HEADS), f32, 1.0, 16.0)),
        "d_skip": 1.0 + nrm(ks[13], (DEPTH, SSM_HEADS), 0.02),
        "ssm_norm_w": 1.0 + nrm(ks[14], (DEPTH, D_INNER), 0.02),
        "w_pa": nrm(ks[15], (DEPTH, ATT_WIDTH, D_MODEL), ATT_WIDTH ** -0.5),
        "w_pb": nrm(ks[16], (DEPTH, D_INNER, D_MODEL), D_INNER ** -0.5),
        "w_o": nrm(ks[17], (DEPTH, D_MODEL, D_MODEL), D_MODEL ** -0.5),
        "norm2_w": 1.0 + nrm(ks[18], (DEPTH, D_MODEL), 0.02),
        "w_router_group": nrm(ks[19], (DEPTH, D_MODEL, N_EXPERT_GROUPS), D_MODEL ** -0.5),
        "w_router_expert": nrm(ks[20], (DEPTH, D_MODEL, N_EXPERTS), D_MODEL ** -0.5),
        "w_gate_e": nrm(ks[21], (DEPTH, N_EXPERTS, D_MODEL, D_EXPERT), D_MODEL ** -0.5),
        "w_up_e": nrm(ks[22], (DEPTH, N_EXPERTS, D_MODEL, D_EXPERT), D_MODEL ** -0.5),
        "w_down_e": nrm(ks[23], (DEPTH, N_EXPERTS, D_EXPERT, D_MODEL), D_EXPERT ** -0.5),
        "norm_f_w": 1.0 + nrm(ks[24], (D_MODEL,), 0.02),
    }


def reference(x_prompt, x_sample, cache_k, cache_v, state_ssm, state_conv, page_table,
              norm1_w, w_in, conv_w, conv_b, dt_bias, a_log, d_skip, ssm_norm_w,
              w_pa, w_pb, w_o, norm2_w, w_router_group, w_router_expert,
              w_gate_e, w_up_e, w_down_e, norm_f_w):
    n_pages = PAST_LEN // PAGE_SIZE
    pos_p = jnp.arange(SEQ, dtype=jnp.int32)
    pos_s = PAST_LEN + jnp.arange(DEC_SEQ, dtype=jnp.int32)
    hp, hs = x_prompt, x_sample
    kp_l, vp_l, sp_l, cp_l, ks_l, vs_l, ss_l, cs_l = [], [], [], [], [], [], [], []
    for l in range(DEPTH):
        lw = (norm1_w[l], w_in[l], conv_w[l], conv_b[l], dt_bias[l], a_log[l], d_skip[l], ssm_norm_w[l],
              w_pa[l], w_pb[l], w_o[l], norm2_w[l], w_router_group[l], w_router_expert[l],
              w_gate_e[l], w_up_e[l], w_down_e[l])
        h0_p = jnp.zeros((BATCH, SSM_HEADS, SSM_HEAD_DIM, D_STATE), state_ssm.dtype)
        pre_p = jnp.zeros((BATCH, CONV_WIDTH - 1, CONV_CH), x_prompt.dtype)
        hp, kn, vn, sn, cn = decoder_layer(hp, pos_p, None, None, h0_p, pre_p, lw)
        kp_l.append(kn); vp_l.append(vn); sp_l.append(sn); cp_l.append(cn)
        k_past = cache_k[l][page_table].reshape(DEC_BATCH, n_pages * PAGE_SIZE, N_KV_HEADS, HEAD_DIM)
        v_past = cache_v[l][page_table].reshape(DEC_BATCH, n_pages * PAGE_SIZE, N_KV_HEADS, HEAD_DIM)
        hs, kn, vn, sn, cn = decoder_layer(hs, pos_s, k_past, v_past, state_ssm[l], state_conv[l], lw)
        ks_l.append(kn); vs_l.append(vn); ss_l.append(sn); cs_l.append(cn)
    y_prompt = rmsnorm(hp, norm_f_w)
    y_sample = rmsnorm(hs, norm_f_w)
    return (y_prompt, y_sample,
            jnp.stack(kp_l), jnp.stack(vp_l), jnp.stack(sp_l), jnp.stack(cp_l),
            jnp.stack(ks_l), jnp.stack(vs_l), jnp.stack(ss_l), jnp.stack(cs_l))
```

```python
import functools
import math

import numpy as np
import jax
import jax.numpy as jnp
from jax import lax
from jax.experimental import pallas as pl
from jax.experimental.pallas import tpu as pltpu

N_HEADS = 16
HEAD_DIM = 64
N_KV_HEADS = 4
Q_PER_KV = N_HEADS // N_KV_HEADS
ATT_WIDTH = N_HEADS * HEAD_DIM
KV_WIDTH = N_KV_HEADS * HEAD_DIM
MOBA_BLOCK = 256
MOBA_TOPK = 3
ROPE_THETA = 10000.0
ATT_SCALE = 1.0 / math.sqrt(HEAD_DIM)
SSM_HEAD_DIM = 64
SSM_GROUPS = 4
D_STATE = 128
CONV_WIDTH = 4
SSD_CHUNK = 128
N_EXPERT_GROUPS = 4
EXPERTS_PER_GROUP = 4
N_EXPERTS = N_EXPERT_GROUPS * EXPERTS_PER_GROUP
RMS_EPS = 1e-6
PAGE_SIZE = 128

LANES = 128
SUBLANES = 8
V7X_VMEM_BYTES = 64 * 1024 * 1024
VMEM_COMPILER_RESERVE = 8 * 1024 * 1024

F32 = jnp.float32
BF16 = jnp.bfloat16
NEG_INF = float("-inf")


def _vmem_limit(block_bytes, scratch_bytes, temp_bytes):
    need = 2 * block_bytes + scratch_bytes + temp_bytes + VMEM_COMPILER_RESERVE
    return int(min(need, V7X_VMEM_BYTES - VMEM_COMPILER_RESERVE))


def _nbytes(shape, dtype):
    return int(np.prod(shape)) * jnp.dtype(dtype).itemsize


def _mm(a, b, precise):
    if precise:
        return jnp.dot(a.astype(F32), b.astype(F32), precision=lax.Precision.HIGHEST,
                       preferred_element_type=F32)
    return jnp.dot(a.astype(BF16), b.astype(BF16), preferred_element_type=F32)


def _mm_nt(a, b, precise):
    dn = (((1,), (1,)), ((), ()))
    if precise:
        return lax.dot_general(a.astype(F32), b.astype(F32), dn, precision=lax.Precision.HIGHEST,
                               preferred_element_type=F32)
    return lax.dot_general(a.astype(BF16), b.astype(BF16), dn, preferred_element_type=F32)


def _sigmoid(x):
    return 1.0 / (1.0 + jnp.exp(-x))


def _silu(x):
    return x * _sigmoid(x)


IN_TN = 512
IN_Q_TILES = ATT_WIDTH // IN_TN
IN_KV_TILE = IN_Q_TILES


def _inproj_kernel(x_ref, nw_ref, w_ref, wdt_ref, cos_ref, sin_ref,
                   qt_ref, k_ref, v_ref, vt_ref, z_ref, xbc_ref, ga_ref, gb_ref, dt_ref,
                   h_scr, *, precise, seg):
    j = pl.program_id(1)
    tm = x_ref.shape[0]

    @pl.when(j == 0)
    def _():
        x = x_ref[...]
        h = x * lax.rsqrt(jnp.mean(x * x, axis=-1, keepdims=True) + RMS_EPS)
        h_scr[...] = (h * nw_ref[...]).astype(h_scr.dtype)

    acc = _mm(h_scr[...], w_ref[...], precise)

    def rope(a):
        cos = cos_ref[...]
        sin = sin_ref[...]
        lane = lax.broadcasted_iota(jnp.int32, (tm, LANES), 1)
        first_half = (lane % HEAD_DIM) < (HEAD_DIM // 2)
        outs = []
        for c in range(a.shape[1] // LANES):
            xc = a[:, c * LANES:(c + 1) * LANES]
            rot = jnp.where(first_half, pltpu.roll(xc, LANES - HEAD_DIM // 2, 1),
                            pltpu.roll(xc, HEAD_DIM // 2, 1))
            outs.append(xc * cos + rot * sin)
        return jnp.concatenate(outs, axis=1)

    @pl.when(j < seg["kv"])
    def _():
        qr = rope(acc) * ATT_SCALE
        qt_ref[0] = qr.T.astype(qt_ref.dtype)

    @pl.when(j == seg["kv"])
    def _():
        k_ref[...] = rope(acc[:, :KV_WIDTH])
        v = acc[:, KV_WIDTH:]
        v_ref[...] = v
        vt_ref[0] = v.T.astype(vt_ref.dtype)
        dt_ref[...] = _mm(h_scr[...], wdt_ref[...], precise)

    @pl.when((j >= seg["z"]) & (j < seg["xbc"]))
    def _():
        z_ref[...] = acc

    @pl.when((j >= seg["xbc"]) & (j < seg["ga"]))
    def _():
        xbc_ref[...] = acc

    @pl.when((j >= seg["ga"]) & (j < seg["gb"]))
    def _():
        ga_ref[...] = acc

    @pl.when(j >= seg["gb"])
    def _():
        gb_ref[...] = acc


def _inproj(x2d, n_batch, norm_w, w_main, w_dt, cos_t, sin_t, *, precise, d_inner, conv_ch):
    T, D = x2d.shape
    S = T // n_batch
    tm = min(1024, S)
    assert S % tm == 0 and ATT_WIDTH % IN_TN == 0 and 2 * KV_WIDTH == IN_TN
    tiles_per_b = S // tm
    nz, nx, ng = d_inner // IN_TN, conv_ch // IN_TN, D // IN_TN
    assert d_inner % IN_TN == 0 and conv_ch % IN_TN == 0 and D % IN_TN == 0
    seg = {"kv": IN_KV_TILE, "z": IN_KV_TILE + 1}
    seg["xbc"] = seg["z"] + nz
    seg["ga"] = seg["xbc"] + nx
    seg["gb"] = seg["ga"] + ng
    n_col = seg["gb"] + ng
    assert w_main.shape == (D, n_col * IN_TN)
    cdt = F32 if precise else BF16

    def clampj(lo, n):
        return lambda i, j: (i, jnp.clip(j - lo, 0, n - 1))

    in_specs = [
        pl.BlockSpec((tm, D), lambda i, j: (i, 0)),
        pl.BlockSpec((1, D), lambda i, j: (0, 0)),
        pl.BlockSpec((D, IN_TN), lambda i, j: (0, j)),
        pl.BlockSpec((D, LANES), lambda i, j: (0, 0)),
        pl.BlockSpec((tm, LANES), lambda i, j: (i % tiles_per_b, 0)),
        pl.BlockSpec((tm, LANES), lambda i, j: (i % tiles_per_b, 0)),
    ]
    out_shape = [
        jax.ShapeDtypeStruct((n_batch, ATT_WIDTH, S), cdt),
        jax.ShapeDtypeStruct((T, KV_WIDTH), F32),
        jax.ShapeDtypeStruct((T, KV_WIDTH), F32),
        jax.ShapeDtypeStruct((n_batch, KV_WIDTH, S), cdt),
        jax.ShapeDtypeStruct((T, d_inner), F32),
        jax.ShapeDtypeStruct((T, conv_ch), F32),
        jax.ShapeDtypeStruct((T, D), F32),
        jax.ShapeDtypeStruct((T, D), F32),
        jax.ShapeDtypeStruct((T, LANES), F32),
    ]
    out_specs = [
        pl.BlockSpec((1, IN_TN, tm),
                     lambda i, j: (i // tiles_per_b, jnp.clip(j, 0, IN_Q_TILES - 1), i % tiles_per_b)),
        pl.BlockSpec((tm, KV_WIDTH), lambda i, j: (i, 0)),
        pl.BlockSpec((tm, KV_WIDTH), lambda i, j: (i, 0)),
        pl.BlockSpec((1, KV_WIDTH, tm), lambda i, j: (i // tiles_per_b, 0, i % tiles_per_b)),
        pl.BlockSpec((tm, IN_TN), clampj(seg["z"], nz)),
        pl.BlockSpec((tm, IN_TN), clampj(seg["xbc"], nx)),
        pl.BlockSpec((tm, IN_TN), clampj(seg["ga"], ng)),
        pl.BlockSpec((tm, IN_TN), clampj(seg["gb"], ng)),
        pl.BlockSpec((tm, LANES), lambda i, j: (i, 0)),
    ]
    blk = (_nbytes((tm, D), F32) + _nbytes((D, IN_TN), cdt) + _nbytes((D, LANES), cdt)
           + 2 * _nbytes((tm, LANES), F32) + _nbytes((IN_TN, tm), cdt) + 2 * _nbytes((tm, KV_WIDTH), F32)
           + _nbytes((KV_WIDTH, tm), cdt) + 4 * _nbytes((tm, IN_TN), F32) + _nbytes((tm, LANES), F32))
    scr = _nbytes((tm, D), cdt)
    tmp = 4 * _nbytes((tm, IN_TN), F32)
    return pl.pallas_call(
        functools.partial(_inproj_kernel, precise=precise, seg=seg),
        grid=(T // tm, n_col),
        in_specs=in_specs,
        out_specs=out_specs,
        out_shape=out_shape,
        scratch_shapes=[pltpu.VMEM((tm, D), cdt)],
        compiler_params=pltpu.CompilerParams(
            dimension_semantics=("parallel", "arbitrary"),
            vmem_limit_bytes=_vmem_limit(blk, scr, tmp)),
        name="inproj",
    )(x2d, norm_w.reshape(1, D), w_main, w_dt, cos_t, sin_t)


def _moba_prompt_kernel(it_ref, jt_ref, qt_ref, k_ref, vt_ref, o_ref,
                        acc_scr, m_scr, l_scr, bias_scr, kmean_scr, *, precise, nb_pad):
    t = pl.program_id(1)
    i = it_ref[t]
    j = jt_ref[t]
    is_diag = j == i
    is_last = (j == i - 1) | (i == 0)
    blk = MOBA_BLOCK
    kf = k_ref[0]

    @pl.when(is_diag)
    def _():
        @pl.when(i == 0)
        def _():
            kmean_scr[...] = jnp.zeros_like(kmean_scr)

        kmean_scr[pl.ds(i, 1), :] = jnp.mean(kf, axis=0, keepdims=True)
        m_scr[...] = jnp.full_like(m_scr, NEG_INF)
        l_scr[...] = jnp.zeros_like(l_scr)
        acc_scr[...] = jnp.zeros_like(acc_scr)
        n_iota = lax.broadcasted_iota(jnp.int32, (nb_pad, blk), 0)
        valid = n_iota < i
        kmean = kmean_scr[...]
        for h in range(N_HEADS):
            g = h // Q_PER_KV
            gt = _mm(kmean[:, g * HEAD_DIM:(g + 1) * HEAD_DIM],
                     qt_ref[0, h * HEAD_DIM:(h + 1) * HEAD_DIM, :], precise)
            cnt = jnp.zeros((nb_pad, blk), jnp.int32)
            for n2 in range(nb_pad):
                row = gt[n2:n2 + 1, :]
                beats = (row > gt) | ((row == gt) & (n2 < n_iota))
                cnt = cnt + jnp.where(beats & (n2 < i), 1, 0)
            bias_scr[h] = jnp.where(valid & (cnt < MOBA_TOPK), 0.0, NEG_INF)

    key_iota = lax.broadcasted_iota(jnp.int32, (blk, blk), 0)
    qry_iota = lax.broadcasted_iota(jnp.int32, (blk, blk), 1)
    causal_bias = jnp.where(key_iota <= qry_iota, 0.0, NEG_INF)
    cdt = F32 if precise else BF16
    for h in range(N_HEADS):
        g = h // Q_PER_KV
        hs = slice(h * HEAD_DIM, (h + 1) * HEAD_DIM)
        gs = slice(g * HEAD_DIM, (g + 1) * HEAD_DIM)
        s = _mm(kf[:, gs], qt_ref[0, hs, :], precise)
        row_bias = bias_scr[h, pl.ds(j, 1), :]
        s = s + jnp.where(is_diag, causal_bias, row_bias)
        m_prev = m_scr[h:h + 1, :]
        m_new = jnp.maximum(m_prev, jnp.max(s, axis=0, keepdims=True))
        alpha = jnp.exp(m_prev - m_new)
        p = jnp.exp(s - m_new)
        l_scr[h:h + 1, :] = alpha * l_scr[h:h + 1, :] + jnp.sum(p, axis=0, keepdims=True)
        m_scr[h:h + 1, :] = m_new
        pv = _mm(vt_ref[0, gs, :], p.astype(cdt), precise)
        acc_scr[hs, :] = alpha * acc_scr[hs, :] + pv

    @pl.when(is_last)
    def _():
        for h in range(N_HEADS):
            hs = slice(h * HEAD_DIM, (h + 1) * HEAD_DIM)
            acc_scr[hs, :] = acc_scr[hs, :] / l_scr[h:h + 1, :]
        o_ref[0] = acc_scr[...].T.astype(o_ref.dtype)


def _moba_prompt(qt, k3, vt, *, precise):
    B, _, S = qt.shape
    blk = MOBA_BLOCK
    assert S % blk == 0
    nb = S // blk
    nb_pad = -(-nb // SUBLANES) * SUBLANES
    it, jt = [], []
    for i in range(nb):
        it += [i] * (i + 1)
        jt += [i] + list(range(i))
    it = jnp.asarray(np.array(it, np.int32))
    jt = jnp.asarray(np.array(jt, np.int32))
    n_steps = int(it.shape[0])
    cdt = F32 if precise else BF16
    grid_spec = pltpu.PrefetchScalarGridSpec(
        num_scalar_prefetch=2,
        grid=(B, n_steps),
        in_specs=[
            pl.BlockSpec((1, ATT_WIDTH, blk), lambda b, t, it, jt: (b, 0, it[t])),
            pl.BlockSpec((1, blk, KV_WIDTH), lambda b, t, it, jt: (b, jt[t], 0)),
            pl.BlockSpec((1, KV_WIDTH, blk), lambda b, t, it, jt: (b, 0, jt[t])),
        ],
        out_specs=pl.BlockSpec((1, blk, ATT_WIDTH), lambda b, t, it, jt: (b, it[t], 0)),
        scratch_shapes=[
            pltpu.VMEM((ATT_WIDTH, blk), F32),
            pltpu.VMEM((N_HEADS, blk), F32),
            pltpu.VMEM((N_HEADS, blk), F32),
            pltpu.VMEM((N_HEADS, nb_pad, blk), F32),
            pltpu.VMEM((nb_pad, KV_WIDTH), F32),
        ],
    )
    blkb = (_nbytes((ATT_WIDTH, blk), cdt) + _nbytes((blk, KV_WIDTH), F32) + _nbytes((KV_WIDTH, blk), cdt)
            + _nbytes((blk, ATT_WIDTH), cdt))
    scr = (_nbytes((ATT_WIDTH, blk), F32) + 2 * _nbytes((N_HEADS, blk), F32)
           + _nbytes((N_HEADS, nb_pad, blk), F32) + _nbytes((nb_pad, KV_WIDTH), F32))
    tmp = 8 * _nbytes((blk, blk), F32) + _nbytes((ATT_WIDTH, blk), F32)
    return pl.pallas_call(
        functools.partial(_moba_prompt_kernel, precise=precise, nb_pad=nb_pad),
        grid_spec=grid_spec,
        out_shape=jax.ShapeDtypeStruct((B, S, ATT_WIDTH), cdt),
        compiler_params=pltpu.CompilerParams(
            dimension_semantics=("parallel", "arbitrary"),
            vmem_limit_bytes=_vmem_limit(blkb, scr, tmp)),
        name="moba_prompt",
    )(it, jt, qt, k3, vt)


PAGES_PER_BLOCK = MOBA_BLOCK // PAGE_SIZE
SAMPLE_BLOCKS_PER_STEP = 4


def _moba_sample_kernel(pt_ref, q_ref, *rest, n_blocks, n_steps, dec_seq):
    del pt_ref
    npg = SAMPLE_BLOCKS_PER_STEP * PAGES_PER_BLOCK
    k_pages = rest[:npg]
    v_pages = rest[npg:2 * npg]
    knew_ref, vnew_ref, o_ref, m_scr, l_scr, g_scr, o_scr = rest[2 * npg:]
    s_id = pl.program_id(1)
    qbd = q_ref[0]
    rows = qbd.shape[0]

    for kk in range(SAMPLE_BLOCKS_PER_STEP):
        n = s_id * SAMPLE_BLOCKS_PER_STEP + kk
        kb = jnp.concatenate([k_pages[kk * PAGES_PER_BLOCK + p][...] for p in range(PAGES_PER_BLOCK)], axis=0)
        vb = jnp.concatenate([v_pages[kk * PAGES_PER_BLOCK + p][...] for p in range(PAGES_PER_BLOCK)], axis=0)
        kmean = jnp.mean(kb, axis=0, keepdims=True)
        g_scr[n] = jnp.sum(qbd * kmean, axis=1, keepdims=True)
        sc = _mm_nt(qbd, kb, True)
        m = jnp.max(sc, axis=1, keepdims=True)
        p = jnp.exp(sc - m)
        m_scr[n] = m
        l_scr[n] = jnp.sum(p, axis=1, keepdims=True)
        o_scr[n] = _mm(p, vb, True)

    @pl.when(s_id == n_steps - 1)
    def _():
        kn = knew_ref[0]
        vn = vnew_ref[0]
        qidx = lax.broadcasted_iota(jnp.int32, (rows, 1), 0) % dec_seq
        cols = []
        for tt in range(dec_seq):
            c = jnp.sum(qbd * kn[tt:tt + 1, :], axis=1, keepdims=True)
            cols.append(jnp.where(tt <= qidx, c, NEG_INF))
        m_own = cols[0]
        for c in cols[1:]:
            m_own = jnp.maximum(m_own, c)
        l_own = jnp.zeros((rows, 1), F32)
        o_own = jnp.zeros((rows, KV_WIDTH), F32)
        for tt in range(dec_seq):
            pt = jnp.exp(cols[tt] - m_own)
            l_own = l_own + pt
            o_own = o_own + pt * vn[tt:tt + 1, :]
        lane = lax.broadcasted_iota(jnp.int32, (rows, LANES), 1)
        gmat = jnp.zeros((rows, LANES), F32)
        mmat = jnp.zeros((rows, LANES), F32)
        lmat = jnp.zeros((rows, LANES), F32)
        for n in range(n_blocks):
            gmat = jnp.where(lane == n, g_scr[n], gmat)
            mmat = jnp.where(lane == n, m_scr[n], mmat)
            lmat = jnp.where(lane == n, l_scr[n], lmat)
        cnt = jnp.zeros((rows, LANES), jnp.int32)
        for n2 in range(n_blocks):
            col = g_scr[n2]
            beats = (col > gmat) | ((col == gmat) & (n2 < lane))
            cnt = cnt + jnp.where(beats, 1, 0)
        sel = (lane < n_blocks) & (cnt < MOBA_TOPK)
        m_all = jnp.maximum(m_own, jnp.max(jnp.where(sel, mmat, NEG_INF), axis=1, keepdims=True))
        wmat = jnp.where(sel, jnp.exp(mmat - m_all), 0.0)
        w_own = jnp.exp(m_own - m_all)
        denom = w_own * l_own + jnp.sum(wmat * lmat, axis=1, keepdims=True)
        out = w_own * o_own
        for n in range(n_blocks):
            out = out + wmat[:, n:n + 1] * o_scr[n]
        o_ref[0] = out / denom


def _moba_sample(qbd, cache_k, cache_v, page_table, k_new, v_new):
    Bd, rows, _ = qbd.shape
    dec_seq = k_new.shape[1]
    n_pages = page_table.shape[1]
    assert (n_pages * PAGE_SIZE) % MOBA_BLOCK == 0
    n_blocks = n_pages // PAGES_PER_BLOCK
    assert n_blocks % SAMPLE_BLOCKS_PER_STEP == 0 and n_blocks <= LANES
    n_steps = n_blocks // SAMPLE_BLOCKS_PER_STEP
    npg = SAMPLE_BLOCKS_PER_STEP * PAGES_PER_BLOCK

    def page_spec(p):
        return pl.BlockSpec((None, PAGE_SIZE, KV_WIDTH), lambda b, s, pt: (pt[b, s * npg + p], 0, 0))

    grid_spec = pltpu.PrefetchScalarGridSpec(
        num_scalar_prefetch=1,
        grid=(Bd, n_steps),
        in_specs=([pl.BlockSpec((1, rows, KV_WIDTH), lambda b, s, pt: (b, 0, 0))]
                  + [page_spec(p) for p in range(npg)] + [page_spec(p) for p in range(npg)]
                  + [pl.BlockSpec((1, dec_seq, KV_WIDTH), lambda b, s, pt: (b, 0, 0))] * 2),
        out_specs=pl.BlockSpec((1, rows, KV_WIDTH), lambda b, s, pt: (b, 0, 0)),
        scratch_shapes=[
            pltpu.VMEM((n_blocks, rows, 1), F32),
            pltpu.VMEM((n_blocks, rows, 1), F32),
            pltpu.VMEM((n_blocks, rows, 1), F32),
            pltpu.VMEM((n_blocks, rows, KV_WIDTH), F32),
        ],
    )
    blkb = (2 * _nbytes((rows, KV_WIDTH), F32) + 2 * npg * _nbytes((PAGE_SIZE, KV_WIDTH), F32)
            + 2 * _nbytes((SUBLANES, KV_WIDTH), F32))
    scr = 3 * _nbytes((n_blocks, rows, LANES), F32) + _nbytes((n_blocks, rows, KV_WIDTH), F32)
    tmp = 8 * _nbytes((MOBA_BLOCK, KV_WIDTH), F32)
    return pl.pallas_call(
        functools.partial(_moba_sample_kernel, n_blocks=n_blocks, n_steps=n_steps, dec_seq=dec_seq),
        grid_spec=grid_spec,
        out_shape=jax.ShapeDtypeStruct((Bd, rows, KV_WIDTH), F32),
        compiler_params=pltpu.CompilerParams(
            dimension_semantics=("parallel", "arbitrary"),
            vmem_limit_bytes=_vmem_limit(blkb, scr, tmp)),
        name="moba_sample",
    )(page_table, qbd, *([cache_k] * npg), *([cache_v] * npg), k_new, v_new)


CONV_PAD = SUBLANES


def _ssd_kernel(*refs, precise, n_valid, n_chunks, has_state, n_ssm_heads, d_inner):
    tc = SSD_CHUNK
    refs = list(refs)
    z_ref, xbc_ref, dt_ref = refs[:3]
    del refs[:3]
    if has_state:
        h0_ref, pre_ref = refs[:2]
        del refs[:2]
    (cw_ref, cb_ref, dtb_ref, a_ref, dsk_ref, nw_ref, o_ref, st_ref, cv_ref,
     ext_scr, h_scr, xt_scr, xd_scr, yt_scr) = refs[:14]
    if n_valid < tc:
        dtp_scr, zp_scr = refs[14:]
    c = pl.program_id(1)
    gw = SSM_GROUPS * D_STATE
    heads_per_group = n_ssm_heads // SSM_GROUPS
    rows_per_group = heads_per_group * SSM_HEAD_DIM

    @pl.when(c == 0)
    def _():
        ext_scr[...] = jnp.zeros_like(ext_scr)
        if has_state:
            ext_scr[CONV_PAD - (CONV_WIDTH - 1):CONV_PAD, :] = pre_ref[0]
            h_scr[...] = h0_ref[0]
        else:
            h_scr[...] = jnp.zeros_like(h_scr)

    ext_scr[CONV_PAD:CONV_PAD + n_valid, :] = xbc_ref[0]
    conv = cb_ref[...]
    for w in range(CONV_WIDTH):
        off = CONV_PAD - (CONV_WIDTH - 1) + w
        conv = conv + ext_scr[off:off + tc, :] * cw_ref[w:w + 1, :]
    xc = _silu(conv)

    @pl.when(c == n_chunks - 1)
    def _():
        lo = CONV_PAD + n_valid - (CONV_WIDTH - 1)
        cv_ref[0] = ext_scr[lo:lo + CONV_WIDTH - 1, :]

    ext_scr[CONV_PAD - (CONV_WIDTH - 1):CONV_PAD, :] = ext_scr[CONV_PAD + tc - (CONV_WIDTH - 1):CONV_PAD + tc, :]

    if n_valid == tc:
        dt_raw = dt_ref[0]
    else:
        dtp_scr[...] = jnp.zeros_like(dtp_scr)
        dtp_scr[0:n_valid, :] = dt_ref[0]
        dt_raw = dtp_scr[...]
    xdt_in = dt_raw + dtb_ref[...]
    dt = jnp.maximum(xdt_in, 0.0) + jnp.log1p(jnp.exp(-jnp.abs(xdt_in)))
    t_iota = lax.broadcasted_iota(jnp.int32, (tc, LANES), 0)
    dt = jnp.where(t_iota < n_valid, dt, 0.0)
    d_a = dt * a_ref[...]
    tri = (lax.broadcasted_iota(jnp.int32, (tc, tc), 0) >= lax.broadcasted_iota(jnp.int32, (tc, tc), 1))
    acum = jnp.dot(tri.astype(F32), d_a, precision=lax.Precision.HIGHEST, preferred_element_type=F32)
    dt_t = dt.T
    acum_t = acum.T
    a_last = acum_t[:, tc - 1:tc]
    exp_a_t = jnp.exp(acum_t)
    dec_t = jnp.exp(a_last - acum_t)
    chunk_dec = jnp.broadcast_to(jnp.exp(a_last), (LANES, D_STATE))

    xt_scr[...] = xc[:, :d_inner].T
    s_iota = lax.broadcasted_iota(jnp.int32, (tc, tc), 0)
    u_iota = lax.broadcasted_iota(jnp.int32, (tc, tc), 1)
    causal = s_iota <= u_iota
    for g in range(SSM_GROUPS):
        bg = xc[:, d_inner + g * D_STATE:d_inner + (g + 1) * D_STATE]
        cg = xc[:, d_inner + gw + g * D_STATE:d_inner + gw + (g + 1) * D_STATE]
        cbt = _mm_nt(bg, cg, precise)
        rs = slice(g * rows_per_group, (g + 1) * rows_per_group)
        hg = h_scr[rs, :]
        yoff = _mm_nt(hg, cg, precise)
        for hh in range(heads_per_group):
            h = g * heads_per_group + hh
            hs = slice(h * SSM_HEAD_DIM, (h + 1) * SSM_HEAD_DIM)
            xh = xt_scr[hs, :]
            xdt = xh * dt_t[h:h + 1, :]
            seg = acum_t[h:h + 1, :] - acum[:, h:h + 1]
            lmat = jnp.exp(jnp.where(causal, seg, NEG_INF))
            yd = _mm(xdt, cbt * lmat, precise)
            yt_scr[hs, :] = (yd + yoff[hh * SSM_HEAD_DIM:(hh + 1) * SSM_HEAD_DIM, :] * exp_a_t[h:h + 1, :]
                             + dsk_ref[h:h + 1, :] * xh)
            xd_scr[hs, :] = xdt * dec_t[h:h + 1, :]
        st = _mm(xd_scr[rs, :], bg, precise)
        for hh in range(heads_per_group):
            h = g * heads_per_group + hh
            hs = slice(h * SSM_HEAD_DIM, (h + 1) * SSM_HEAD_DIM)
            lo = hh * SSM_HEAD_DIM
            h_scr[hs, :] = hg[lo:lo + SSM_HEAD_DIM, :] * chunk_dec[h:h + 1, :] + st[lo:lo + SSM_HEAD_DIM, :]

    @pl.when(c == n_chunks - 1)
    def _():
        st_ref[0] = h_scr[...]

    y = yt_scr[...].T
    if n_valid == tc:
        zz = z_ref[0]
    else:
        zp_scr[...] = jnp.zeros_like(zp_scr)
        zp_scr[0:n_valid, :] = z_ref[0]
        zz = zp_scr[...]
    yz = y * _silu(zz)
    gsz = d_inner // SSM_GROUPS
    outs = []
    for g in range(SSM_GROUPS):
        grp = yz[:, g * gsz:(g + 1) * gsz]
        outs.append(grp * lax.rsqrt(jnp.mean(grp * grp, axis=-1, keepdims=True) + RMS_EPS))
    res = jnp.concatenate(outs, axis=1) * nw_ref[...]
    o_ref[0] = res[:n_valid, :].astype(o_ref.dtype)


def _ssd(z3, xbc3, dt3, h0, prefix, conv_w, conv_b, dt_bias, a_log, d_skip, ssm_norm_w, *, precise):
    B, S, d_inner = z3.shape
    conv_ch = xbc3.shape[2]
    n_heads = d_inner // SSM_HEAD_DIM
    tc = SSD_CHUNK
    if S % tc == 0:
        n_valid, n_chunks = tc, S // tc
    else:
        assert S < tc
        n_valid, n_chunks = S, 1
    has_state = h0 is not None
    cdt = F32 if precise else BF16
    pad_h = LANES - n_heads
    dtb = jnp.pad(dt_bias.astype(F32), (0, pad_h)).reshape(1, LANES)
    a_row = jnp.pad(-jnp.exp(a_log.astype(F32)), (0, pad_h)).reshape(1, LANES)
    dsk = jnp.broadcast_to(jnp.pad(d_skip.astype(F32), (0, pad_h))[:, None], (LANES, tc))

    full2 = lambda shp: pl.BlockSpec(shp, lambda b, c: (0,) * len(shp))
    in_specs = [
        pl.BlockSpec((1, n_valid, d_inner), lambda b, c: (b, c, 0)),
        pl.BlockSpec((1, n_valid, conv_ch), lambda b, c: (b, c, 0)),
        pl.BlockSpec((1, n_valid, LANES), lambda b, c: (b, c, 0)),
    ]
    args = [z3, xbc3, dt3]
    if has_state:
        in_specs += [pl.BlockSpec((1, d_inner, D_STATE), lambda b, c: (b, 0, 0)),
                     pl.BlockSpec((1, CONV_WIDTH - 1, conv_ch), lambda b, c: (b, 0, 0))]
        args += [h0, prefix]
    in_specs += [full2((CONV_WIDTH, conv_ch)), full2((1, conv_ch)), full2((1, LANES)), full2((1, LANES)),
                 full2((LANES, tc)), full2((1, d_inner))]
    args += [conv_w, conv_b.reshape(1, conv_ch), dtb, a_row, dsk, ssm_norm_w.reshape(1, d_inner)]
    out_shape = [jax.ShapeDtypeStruct((B, S, d_inner), cdt),
                 jax.ShapeDtypeStruct((B, d_inner, D_STATE), F32),
                 jax.ShapeDtypeStruct((B, CONV_WIDTH - 1, conv_ch), F32)]
    out_specs = [pl.BlockSpec((1, n_valid, d_inner), lambda b, c: (b, c, 0)),
                 pl.BlockSpec((1, d_inner, D_STATE), lambda b, c: (b, 0, 0)),
                 pl.BlockSpec((1, CONV_WIDTH - 1, conv_ch), lambda b, c: (b, 0, 0))]
    scratch = [pltpu.VMEM((CONV_PAD + tc, conv_ch), F32), pltpu.VMEM((d_inner, D_STATE), F32),
               pltpu.VMEM((d_inner, tc), F32), pltpu.VMEM((d_inner, tc), F32), pltpu.VMEM((d_inner, tc), F32)]
    if n_valid < tc:
        scratch += [pltpu.VMEM((tc, LANES), F32), pltpu.VMEM((tc, d_inner), F32)]
    blk = (_nbytes((tc, d_inner), F32) + _nbytes((tc, conv_ch), F32) + _nbytes((tc, LANES), F32)
           + 2 * _nbytes((d_inner, D_STATE), F32) + 2 * _nbytes((SUBLANES, conv_ch), F32)
           + _nbytes((CONV_WIDTH + SUBLANES, conv_ch), F32) + _nbytes((tc, d_inner), cdt))
    scr = _nbytes((CONV_PAD + tc, conv_ch), F32) + 4 * _nbytes((d_inner, D_STATE), F32)
    tmp = 3 * _nbytes((tc, conv_ch), F32) + 4 * _nbytes((tc, d_inner), F32)
    return pl.pallas_call(
        functools.partial(_ssd_kernel, precise=precise, n_valid=n_valid, n_chunks=n_chunks,
                          has_state=has_state, n_ssm_heads=n_heads, d_inner=d_inner),
        grid=(B, n_chunks),
        in_specs=in_specs,
        out_specs=out_specs,
        out_shape=out_shape,
        scratch_shapes=scratch,
        compiler_params=pltpu.CompilerParams(
            dimension_semantics=("parallel", "arbitrary"),
            vmem_limit_bytes=_vmem_limit(blk, scr, tmp)),
        name="ssd",
    )(*args)


def _merge_kernel(x_ref, att_ref, ssd_ref, ga_ref, gb_ref, wpa_ref, wpb_ref, wo_ref, o_ref, *, precise):
    a = _mm(att_ref[...], wpa_ref[...], precise)
    b = _mm(ssd_ref[...], wpb_ref[...], precise)
    m = _sigmoid(ga_ref[...]) * a + _sigmoid(gb_ref[...]) * b
    o_ref[...] = x_ref[...] + _mm(m, wo_ref[...], precise)


def _merge(x2d, att, ssd, ga, gb, w_pa, w_pb, w_o, *, precise):
    T, D = x2d.shape
    d_inner = ssd.shape[1]
    tm = min(512, T)
    assert T % tm == 0
    cdt = F32 if precise else BF16
    row = lambda w: pl.BlockSpec((tm, w), lambda i: (i, 0))
    full = lambda r, w: pl.BlockSpec((r, w), lambda i: (0, 0))
    blk = (4 * _nbytes((tm, D), F32) + _nbytes((tm, ATT_WIDTH), cdt) + _nbytes((tm, d_inner), cdt)
           + _nbytes((ATT_WIDTH + d_inner + D, D), cdt))
    tmp = 4 * _nbytes((tm, D), F32)
    return pl.pallas_call(
        functools.partial(_merge_kernel, precise=precise),
        grid=(T // tm,),
        in_specs=[row(D), row(ATT_WIDTH), row(d_inner), row(D), row(D),
                  full(ATT_WIDTH, D), full(d_inner, D), full(D, D)],
        out_specs=row(D),
        out_shape=jax.ShapeDtypeStruct((T, D), F32),
        compiler_params=pltpu.CompilerParams(
            dimension_semantics=("parallel",),
            vmem_limit_bytes=_vmem_limit(blk, 0, tmp)),
        name="merge",
    )(x2d, att, ssd, ga, gb, w_pa, w_pb, w_o)


def _moe_kernel(x_ref, n2_ref, wr_ref, wg_ref, wu_ref, wd_ref, nf_ref, o_ref, h_scr, gate_scr, acc_scr,
                *, precise, final_norm):
    e = pl.program_id(1)
    tm = x_ref.shape[0]
    epg = EXPERTS_PER_GROUP

    @pl.when(e == 0)
    def _():
        x = x_ref[...]
        h = x * lax.rsqrt(jnp.mean(x * x, axis=-1, keepdims=True) + RMS_EPS) * n2_ref[...]
        h_scr[...] = h.astype(h_scr.dtype)
        acc_scr[...] = jnp.zeros_like(acc_scr)
        lg = _mm(h, wr_ref[...], precise)
        gl = [lg[:, k:k + 1] for k in range(N_EXPERT_GROUPS)]
        gmax = functools.reduce(jnp.maximum, gl)
        taken = jnp.zeros((tm, 1), jnp.bool_)
        is_g = []
        for k in range(N_EXPERT_GROUPS):
            hit = (gl[k] == gmax) & (~taken)
            is_g.append(hit)
            taken = taken | hit
        p_grp = 1.0 / functools.reduce(lambda a, b: a + b, [jnp.exp(v - gmax) for v in gl])
        le = []
        for k in range(epg):
            v = jnp.zeros((tm, 1), F32)
            for gi in range(N_EXPERT_GROUPS):
                c0 = N_EXPERT_GROUPS + gi * epg + k
                v = jnp.where(is_g[gi], lg[:, c0:c0 + 1], v)
            le.append(v)

        def first_max(vals):
            mx = functools.reduce(jnp.maximum, vals)
            tk = jnp.zeros((tm, 1), jnp.bool_)
            hits = []
            for v in vals:
                hit = (v == mx) & (~tk)
                hits.append(hit)
                tk = tk | hit
            return mx, hits

        v1, o1 = first_max(le)
        v2, o2 = first_max([jnp.where(o1[k], NEG_INF, le[k]) for k in range(epg)])
        e21 = jnp.exp(v2 - v1)
        w1 = p_grp / (1.0 + e21)
        w2 = p_grp * e21 / (1.0 + e21)
        lane = lax.broadcasted_iota(jnp.int32, (tm, LANES), 1)
        for gi in range(N_EXPERT_GROUPS):
            gm = jnp.zeros((tm, LANES), F32)
            for k in range(epg):
                wk = jnp.where(o1[k], w1, jnp.where(o2[k], w2, 0.0))
                gm = jnp.where(lane == k, jnp.where(is_g[gi], wk, 0.0), gm)
            gate_scr[gi] = gm

    h = h_scr[...]
    gate = gate_scr[e]
    cdt = F32 if precise else BF16
    acc = acc_scr[...]
    for k in range(epg):
        hg = _mm(h, wg_ref[k], precise)
        hu = _mm(h, wu_ref[k], precise)
        hid = (_silu(hg) * hu * gate[:, k:k + 1]).astype(cdt)
        acc = acc + _mm(hid, wd_ref[k], precise)
    acc_scr[...] = acc

    @pl.when(e == N_EXPERT_GROUPS - 1)
    def _():
        y = x_ref[...] + acc_scr[...]
        if final_norm:
            y = y * lax.rsqrt(jnp.mean(y * y, axis=-1, keepdims=True) + RMS_EPS) * nf_ref[...]
        o_ref[...] = y


def _moe(x2d, norm2_w, w_router, w_g, w_u, w_d, norm_f_w, *, precise, final_norm):
    T, D = x2d.shape
    n_e, _, d_e = w_g.shape
    assert n_e == N_EXPERTS
    tm = min(1024, T)
    assert T % tm == 0
    cdt = F32 if precise else BF16
    epg = EXPERTS_PER_GROUP
    blk = (2 * _nbytes((tm, D), F32) + 2 * _nbytes((1, D), F32) + _nbytes((D, LANES), F32)
           + 3 * _nbytes((epg, D, d_e), cdt))
    scr = _nbytes((tm, D), cdt) + _nbytes((N_EXPERT_GROUPS, tm, LANES), F32) + _nbytes((tm, D), F32)
    tmp = 6 * _nbytes((tm, d_e), F32) + 2 * _nbytes((tm, D), F32) + 64 * _nbytes((tm, LANES), F32)
    return pl.pallas_call(
        functools.partial(_moe_kernel, precise=precise, final_norm=final_norm),
        grid=(T // tm, N_EXPERT_GROUPS),
        in_specs=[
            pl.BlockSpec((tm, D), lambda i, e: (i, 0)),
            pl.BlockSpec((1, D), lambda i, e: (0, 0)),
            pl.BlockSpec((D, LANES), lambda i, e: (0, 0)),
            pl.BlockSpec((epg, D, d_e), lambda i, e: (e, 0, 0)),
            pl.BlockSpec((epg, D, d_e), lambda i, e: (e, 0, 0)),
            pl.BlockSpec((epg, d_e, D), lambda i, e: (e, 0, 0)),
            pl.BlockSpec((1, D), lambda i, e: (0, 0)),
        ],
        out_specs=pl.BlockSpec((tm, D), lambda i, e: (i, 0)),
        out_shape=jax.ShapeDtypeStruct((T, D), F32),
        scratch_shapes=[pltpu.VMEM((tm, D), cdt), pltpu.VMEM((N_EXPERT_GROUPS, tm, LANES), F32),
                        pltpu.VMEM((tm, D), F32)],
        compiler_params=pltpu.CompilerParams(
            dimension_semantics=("parallel", "arbitrary"),
            vmem_limit_bytes=_vmem_limit(blk, scr, tmp)),
        name="moe",
    )(x2d, norm2_w.reshape(1, D), w_router, w_g, w_u, w_d, norm_f_w.reshape(1, D))


def _rope_tables(pos):
    inv_freq = 1.0 / (ROPE_THETA ** (jnp.arange(0, HEAD_DIM, 2, dtype=F32) / HEAD_DIM))
    ang = pos.astype(F32)[:, None] * inv_freq[None, :]
    ang = jnp.concatenate([ang, ang], axis=-1)
    sign = jnp.concatenate([-jnp.ones((HEAD_DIM // 2,), F32), jnp.ones((HEAD_DIM // 2,), F32)])
    reps = LANES // HEAD_DIM
    return jnp.tile(jnp.cos(ang), (1, reps)), jnp.tile(jnp.sin(ang) * sign, (1, reps))


def _layer_weights(lw, precise):
    (norm1_w, w_in, conv_w, conv_b, dt_bias, a_log, d_skip, ssm_norm_w,
     w_pa, w_pb, w_o, norm2_w, w_rg, w_re, w_g, w_u, w_d) = lw
    cdt = F32 if precise else BF16
    D = w_in.shape[0]
    d_inner = ssm_norm_w.shape[0]
    conv_ch = conv_w.shape[1]
    n_heads = dt_bias.shape[0]
    dt0 = ATT_WIDTH + 2 * KV_WIDTH + d_inner + conv_ch
    w_main = jnp.concatenate([w_in[:, :dt0], w_in[:, dt0 + n_heads:]], axis=1).astype(cdt)
    w_dt = jnp.pad(w_in[:, dt0:dt0 + n_heads], ((0, 0), (0, LANES - n_heads))).astype(cdt)
    w_router = jnp.pad(jnp.concatenate([w_rg, w_re], axis=1),
                       ((0, 0), (0, LANES - N_EXPERT_GROUPS - N_EXPERTS))).astype(F32)
    return dict(norm1_w=norm1_w, w_main=w_main, w_dt=w_dt, conv_w=conv_w, conv_b=conv_b, dt_bias=dt_bias,
                a_log=a_log, d_skip=d_skip, ssm_norm_w=ssm_norm_w, w_pa=w_pa.astype(cdt),
                w_pb=w_pb.astype(cdt), w_o=w_o.astype(cdt), norm2_w=norm2_w, w_router=w_router,
                w_g=w_g.astype(cdt), w_u=w_u.astype(cdt), w_d=w_d.astype(cdt),
                d_inner=d_inner, conv_ch=conv_ch)


def _prompt_layer(x3, W, norm_f_w, final_norm):
    B, S, D = x3.shape
    T = B * S
    cos_t, sin_t = _rope_tables(jnp.arange(S, dtype=jnp.int32))
    qt, k, v, vt, z, xbc, ga, gb, dt = _inproj(
        x3.reshape(T, D), B, W["norm1_w"], W["w_main"], W["w_dt"], cos_t, sin_t,
        precise=False, d_inner=W["d_inner"], conv_ch=W["conv_ch"])
    att = _moba_prompt(qt, k.reshape(B, S, KV_WIDTH), vt, precise=False)
    ssd, state, conv_new = _ssd(
        z.reshape(B, S, -1), xbc.reshape(B, S, -1), dt.reshape(B, S, LANES), None, None,
        W["conv_w"], W["conv_b"], W["dt_bias"], W["a_log"], W["d_skip"], W["ssm_norm_w"], precise=False)
    x1 = _merge(x3.reshape(T, D), att.reshape(T, ATT_WIDTH), ssd.reshape(T, -1), ga, gb,
                W["w_pa"], W["w_pb"], W["w_o"], precise=False)
    y = _moe(x1, W["norm2_w"], W["w_router"], W["w_g"], W["w_u"], W["w_d"], norm_f_w,
             precise=False, final_norm=final_norm)
    return y.reshape(B, S, D), k, v, state, conv_new


def _sample_layer(x3, W, norm_f_w, final_norm, cache_k, cache_v, page_table, h0, prefix):
    Bd, Sd, D = x3.shape
    T = Bd * Sd
    past_len = page_table.shape[1] * PAGE_SIZE
    pos = past_len + (jnp.arange(T, dtype=jnp.int32) % Sd)
    cos_t, sin_t = _rope_tables(pos)
    qt, k, v, _, z, xbc, ga, gb, dt = _inproj(
        x3.reshape(T, D), 1, W["norm1_w"], W["w_main"], W["w_dt"], cos_t, sin_t,
        precise=True, d_inner=W["d_inner"], conv_ch=W["conv_ch"])
    q = qt[0].T.reshape(Bd, Sd, N_HEADS, HEAD_DIM).transpose(0, 2, 1, 3)
    grp_of_head = jnp.arange(N_HEADS) // Q_PER_KV
    onehot = (grp_of_head[:, None] == jnp.arange(N_KV_HEADS)[None, :]).astype(F32)
    qbd = (q[:, :, :, None, :] * onehot[None, :, None, :, None]).reshape(Bd, N_HEADS * Sd, KV_WIDTH)
    o = _moba_sample(qbd, cache_k, cache_v, page_table,
                     k.reshape(Bd, Sd, KV_WIDTH), v.reshape(Bd, Sd, KV_WIDTH))
    o = o.reshape(Bd, N_HEADS, Sd, N_KV_HEADS, HEAD_DIM)
    o = jnp.sum(o * onehot[None, :, None, :, None], axis=3)
    att = o.transpose(0, 2, 1, 3).reshape(T, ATT_WIDTH)
    ssd, state, conv_new = _ssd(
        z.reshape(Bd, Sd, -1), xbc.reshape(Bd, Sd, -1), dt.reshape(Bd, Sd, LANES), h0, prefix,
        W["conv_w"], W["conv_b"], W["dt_bias"], W["a_log"], W["d_skip"], W["ssm_norm_w"], precise=True)
    x1 = _merge(x3.reshape(T, D), att, ssd.reshape(T, -1), ga, gb,
                W["w_pa"], W["w_pb"], W["w_o"], precise=True)
    y = _moe(x1, W["norm2_w"], W["w_router"], W["w_g"], W["w_u"], W["w_d"], norm_f_w,
             precise=True, final_norm=final_norm)
    return y.reshape(Bd, Sd, D), k, v, state, conv_new


def kernel(x_prompt, x_sample, cache_k, cache_v, state_ssm, state_conv, page_table, norm1_w, w_in, conv_w,
           conv_b, dt_bias, a_log, d_skip, ssm_norm_w, w_pa, w_pb, w_o, norm2_w, w_router_group,
           w_router_expert, w_gate_e, w_up_e, w_down_e, norm_f_w):
    depth = w_in.shape[0]
    B, S, D = x_prompt.shape
    Bd, Sd, _ = x_sample.shape
    n_phys = cache_k.shape[1]
    hp, hs = x_prompt, x_sample
    outs = [[] for _ in range(8)]
    for l in range(depth):
        lw = (norm1_w[l], w_in[l], conv_w[l], conv_b[l], dt_bias[l], a_log[l], d_skip[l], ssm_norm_w[l],
              w_pa[l], w_pb[l], w_o[l], norm2_w[l], w_router_group[l], w_router_expert[l],
              w_gate_e[l], w_up_e[l], w_down_e[l])
        last = l == depth - 1
        hp, kp, vp, sp, cp = _prompt_layer(hp, _layer_weights(lw, False), norm_f_w, last)
        n_ssm_heads, ssm_hd, d_state = state_ssm.shape[2:]
        hs, ks, vs, ss, cs = _sample_layer(
            hs, _layer_weights(lw, True), norm_f_w, last,
            cache_k[l].reshape(n_phys, PAGE_SIZE, KV_WIDTH), cache_v[l].reshape(n_phys, PAGE_SIZE, KV_WIDTH),
            page_table, state_ssm[l].reshape(Bd, n_ssm_heads * ssm_hd, d_state), state_conv[l])
        outs[0].append(kp.reshape(B, S, N_KV_HEADS, HEAD_DIM))
        outs[1].append(vp.reshape(B, S, N_KV_HEADS, HEAD_DIM))
        outs[2].append(sp.reshape(B, n_ssm_heads, ssm_hd, d_state))
        outs[3].append(cp)
        outs[4].append(ks.reshape(Bd, Sd, N_KV_HEADS, HEAD_DIM))
        outs[5].append(vs.reshape(Bd, Sd, N_KV_HEADS, HEAD_DIM))
        outs[6].append(ss.reshape(Bd, n_ssm_heads, ssm_hd, d_state))
        outs[7].append(cs)
    return (hp, hs) + tuple(jnp.stack(o) for o in outs)
```

```python
import functools
import math

import numpy as np
import jax
import jax.numpy as jnp
from jax import lax
from jax.experimental import pallas as pl
from jax.experimental.pallas import tpu as pltpu

N_HEADS = 16
HEAD_DIM = 64
N_KV_HEADS = 4
Q_PER_KV = N_HEADS // N_KV_HEADS
ATT_WIDTH = N_HEADS * HEAD_DIM
KV_WIDTH = N_KV_HEADS * HEAD_DIM
MOBA_BLOCK = 256
MOBA_TOPK = 3
ROPE_THETA = 10000.0
ATT_SCALE = 1.0 / math.sqrt(HEAD_DIM)
LOG2E = math.log2(math.e)
SSM_HEAD_DIM = 64
SSM_GROUPS = 4
D_STATE = 128
CONV_WIDTH = 4
SSD_CHUNK = 128
N_EXPERT_GROUPS = 4
EXPERTS_PER_GROUP = 4
N_EXPERTS = N_EXPERT_GROUPS * EXPERTS_PER_GROUP
RMS_EPS = 1e-6
PAGE_SIZE = 128

LANES = 128
SUBLANES = 8
V7X_VMEM_BYTES = 64 * 1024 * 1024
VMEM_COMPILER_RESERVE = 8 * 1024 * 1024

F32 = jnp.float32
BF16 = jnp.bfloat16
NEG_INF = float("-inf")


def _vmem_limit(block_bytes, scratch_bytes, temp_bytes):
    need = 2 * block_bytes + scratch_bytes + temp_bytes + VMEM_COMPILER_RESERVE
    return int(min(need, V7X_VMEM_BYTES - VMEM_COMPILER_RESERVE))


def _nbytes(shape, dtype):
    return int(np.prod(shape)) * jnp.dtype(dtype).itemsize


def _mm(a, b, precise):
    if precise:
        return jnp.dot(a.astype(F32), b.astype(F32), precision=lax.Precision.HIGHEST,
                       preferred_element_type=F32)
    return jnp.dot(a.astype(BF16), b.astype(BF16), preferred_element_type=F32)


def _mm_nt(a, b, precise):
    dn = (((1,), (1,)), ((), ()))
    if precise:
        return lax.dot_general(a.astype(F32), b.astype(F32), dn, precision=lax.Precision.HIGHEST,
                               preferred_element_type=F32)
    return lax.dot_general(a.astype(BF16), b.astype(BF16), dn, preferred_element_type=F32)


def _mm_tn(a, b):
    dn = (((0,), (0,)), ((), ()))
    return lax.dot_general(a.astype(BF16), b.astype(BF16), dn, preferred_element_type=F32)


def _sigmoid(x):
    return 1.0 / (1.0 + jnp.exp(-x))


def _silu(x):
    return x * _sigmoid(x)


IN_TN = 512
IN_Q_TILES = ATT_WIDTH // IN_TN
IN_KV_TILE = IN_Q_TILES


def _inproj_kernel(x_ref, nw_ref, w_ref, wdt_ref, cos_ref, sin_ref,
                   q_ref, kt_ref, vt_ref, vtc_ref, z_ref, xbc_ref, ga_ref, gb_ref, dt_ref,
                   h_scr, *, precise, seg, q_scale, q_blocks):
    j = pl.program_id(1)
    tm = x_ref.shape[0]

    @pl.when(j == 0)
    def _():
        x = x_ref[...]
        h = x * lax.rsqrt(jnp.mean(x * x, axis=-1, keepdims=True) + RMS_EPS)
        h_scr[...] = (h * nw_ref[...]).astype(h_scr.dtype)

    acc = _mm(h_scr[...], w_ref[...], precise)

    def rope(a):
        cos = cos_ref[...]
        sin = sin_ref[...]
        lane = lax.broadcasted_iota(jnp.int32, (tm, LANES), 1)
        first_half = (lane % HEAD_DIM) < (HEAD_DIM // 2)
        outs = []
        for c in range(a.shape[1] // LANES):
            xc = a[:, c * LANES:(c + 1) * LANES]
            rot = jnp.where(first_half, pltpu.roll(xc, LANES - HEAD_DIM // 2, 1),
                            pltpu.roll(xc, HEAD_DIM // 2, 1))
            outs.append(xc * cos + rot * sin)
        return jnp.concatenate(outs, axis=1)

    @pl.when(j < seg["kv"])
    def _():
        qr = rope(acc) * q_scale
        if q_blocks:
            qrt = qr.T.astype(q_ref.dtype)
            for hh in range(IN_TN // HEAD_DIM):
                for bi in range(tm // MOBA_BLOCK):
                    q_ref[0, bi, :, hh * MOBA_BLOCK:(hh + 1) * MOBA_BLOCK] = (
                        qrt[hh * HEAD_DIM:(hh + 1) * HEAD_DIM, bi * MOBA_BLOCK:(bi + 1) * MOBA_BLOCK])
        else:
            q_ref[...] = qr.astype(q_ref.dtype)

    @pl.when(j == seg["kv"])
    def _():
        kt_ref[0] = rope(acc[:, :KV_WIDTH]).T
        vt = acc[:, KV_WIDTH:].T
        vt_ref[0] = vt
        vtc_ref[0] = vt.astype(vtc_ref.dtype)
        dt_ref[...] = _mm(h_scr[...], wdt_ref[...], precise)

    @pl.when((j >= seg["z"]) & (j < seg["xbc"]))
    def _():
        z_ref[...] = acc

    @pl.when((j >= seg["xbc"]) & (j < seg["ga"]))
    def _():
        xbc_ref[...] = acc

    @pl.when((j >= seg["ga"]) & (j < seg["gb"]))
    def _():
        ga_ref[...] = acc

    @pl.when(j >= seg["gb"])
    def _():
        gb_ref[...] = acc


def _inproj(x2d, n_batch, norm_w, w_main, w_dt, cos_t, sin_t, *, precise, d_inner, conv_ch, q_scale, q_blocks):
    T, D = x2d.shape
    S = T // n_batch
    tm = min(1024, S)
    assert S % tm == 0 and ATT_WIDTH % IN_TN == 0 and 2 * KV_WIDTH == IN_TN
    assert not q_blocks or tm % MOBA_BLOCK == 0
    tiles_per_b = S // tm
    nz, nx, ng = d_inner // IN_TN, conv_ch // IN_TN, D // IN_TN
    assert d_inner % IN_TN == 0 and conv_ch % IN_TN == 0 and D % IN_TN == 0
    seg = {"kv": IN_KV_TILE, "z": IN_KV_TILE + 1}
    seg["xbc"] = seg["z"] + nz
    seg["ga"] = seg["xbc"] + nx
    seg["gb"] = seg["ga"] + ng
    n_col = seg["gb"] + ng
    assert w_main.shape == (D, n_col * IN_TN)
    cdt = F32 if precise else BF16

    def clampj(lo, n):
        return lambda i, j: (i, jnp.clip(j - lo, 0, n - 1))

    in_specs = [
        pl.BlockSpec((tm, D), lambda i, j: (i, 0)),
        pl.BlockSpec((1, D), lambda i, j: (0, 0)),
        pl.BlockSpec((D, IN_TN), lambda i, j: (0, j)),
        pl.BlockSpec((D, LANES), lambda i, j: (0, 0)),
        pl.BlockSpec((tm, LANES), lambda i, j: (i % tiles_per_b, 0)),
        pl.BlockSpec((tm, LANES), lambda i, j: (i % tiles_per_b, 0)),
    ]
    jq = lambda j: jnp.clip(j, 0, IN_Q_TILES - 1)
    if q_blocks:
        bpt = tm // MOBA_BLOCK
        q_shape = jax.ShapeDtypeStruct((n_batch, S // MOBA_BLOCK, HEAD_DIM, N_HEADS * MOBA_BLOCK), cdt)
        q_spec = pl.BlockSpec((1, bpt, HEAD_DIM, (IN_TN // HEAD_DIM) * MOBA_BLOCK),
                              lambda i, j: (i // tiles_per_b, i % tiles_per_b, 0, jq(j)))
        q_bytes = _nbytes((bpt, HEAD_DIM, (IN_TN // HEAD_DIM) * MOBA_BLOCK), cdt)
    else:
        q_shape = jax.ShapeDtypeStruct((T, ATT_WIDTH), cdt)
        q_spec = pl.BlockSpec((tm, IN_TN), lambda i, j: (i, jq(j)))
        q_bytes = _nbytes((tm, IN_TN), cdt)
    kvt_spec = pl.BlockSpec((1, KV_WIDTH, tm), lambda i, j: (i // tiles_per_b, 0, i % tiles_per_b))
    out_shape = [
        q_shape,
        jax.ShapeDtypeStruct((n_batch, KV_WIDTH, S), F32),
        jax.ShapeDtypeStruct((n_batch, KV_WIDTH, S), F32),
        jax.ShapeDtypeStruct((n_batch, KV_WIDTH, S), cdt),
        jax.ShapeDtypeStruct((T, d_inner), F32),
        jax.ShapeDtypeStruct((T, conv_ch), F32),
        jax.ShapeDtypeStruct((T, D), F32),
        jax.ShapeDtypeStruct((T, D), F32),
        jax.ShapeDtypeStruct((T, LANES), F32),
    ]
    out_specs = [
        q_spec, kvt_spec, kvt_spec, kvt_spec,
        pl.BlockSpec((tm, IN_TN), clampj(seg["z"], nz)),
        pl.BlockSpec((tm, IN_TN), clampj(seg["xbc"], nx)),
        pl.BlockSpec((tm, IN_TN), clampj(seg["ga"], ng)),
        pl.BlockSpec((tm, IN_TN), clampj(seg["gb"], ng)),
        pl.BlockSpec((tm, LANES), lambda i, j: (i, 0)),
    ]
    blk = (_nbytes((tm, D), F32) + _nbytes((D, IN_TN), cdt) + _nbytes((D, LANES), cdt)
           + 2 * _nbytes((tm, LANES), F32) + q_bytes + 2 * _nbytes((tm, KV_WIDTH), F32)
           + _nbytes((KV_WIDTH, tm), cdt) + 4 * _nbytes((tm, IN_TN), F32) + _nbytes((tm, LANES), F32))
    scr = _nbytes((tm, D), cdt)
    tmp = 4 * _nbytes((tm, IN_TN), F32)
    return pl.pallas_call(
        functools.partial(_inproj_kernel, precise=precise, seg=seg, q_scale=q_scale, q_blocks=q_blocks),
        grid=(T // tm, n_col),
        in_specs=in_specs,
        out_specs=out_specs,
        out_shape=out_shape,
        scratch_shapes=[pltpu.VMEM((tm, D), cdt)],
        compiler_params=pltpu.CompilerParams(
            dimension_semantics=("parallel", "arbitrary"),
            vmem_limit_bytes=_vmem_limit(blk, scr, tmp)),
        name="inproj",
    )(x2d, norm_w.reshape(1, D), w_main, w_dt, cos_t, sin_t)


def _moba_prompt_kernel(it_ref, jt_ref, q_ref, kt_ref, vt_ref, o_ref,
                        acc_scr, m_scr, l_scr, sel_scr, kmt_scr, *, nb_pad):
    t = pl.program_id(1)
    i = it_ref[t]
    j = jt_ref[t]
    is_diag = j == i
    is_last = (j == i - 1) | (i == 0)
    blk = MOBA_BLOCK
    gw = Q_PER_KV * blk

    def q_group(g):
        return q_ref[0, 0, :, g * gw:(g + 1) * gw]

    def scores(g):
        return _mm_tn(kt_ref[0, g * HEAD_DIM:(g + 1) * HEAD_DIM, :], q_group(g))

    def v_group(g):
        return vt_ref[0, g * HEAD_DIM:(g + 1) * HEAD_DIM, :]

    @pl.when(is_diag)
    def _():
        @pl.when(i == 0)
        def _():
            kmt_scr[...] = jnp.zeros_like(kmt_scr)

        lane = lax.broadcasted_iota(jnp.int32, kmt_scr.shape, 1)
        kmt_scr[...] = jnp.where(lane == i, jnp.mean(kt_ref[0], axis=1, keepdims=True), kmt_scr[...])
        n_iota = lax.broadcasted_iota(jnp.int32, (nb_pad, gw), 0)
        valid = n_iota < i
        key_iota = lax.broadcasted_iota(jnp.int32, (blk, blk), 0)
        qry_iota = lax.broadcasted_iota(jnp.int32, (blk, blk), 1)
        causal_bias = jnp.where(key_iota <= qry_iota, 0.0, NEG_INF)
        causal_bias = jnp.concatenate([causal_bias] * Q_PER_KV, axis=1)
        s_next = scores(0)
        for g in range(N_KV_HEADS):
            s = s_next
            if g + 1 < N_KV_HEADS:
                s_next = scores(g + 1)
            gt = _mm_tn(kmt_scr[g * HEAD_DIM:(g + 1) * HEAD_DIM, :], q_group(g))[:nb_pad, :]
            cnt = jnp.zeros((nb_pad, gw), jnp.int32)
            for n2 in range(nb_pad):
                row = gt[n2:n2 + 1, :]
                beats = (row > gt) | ((row == gt) & (n2 < n_iota))
                cnt = cnt + jnp.where(beats & (n2 < i), 1, 0)
            sel_scr[:, g * gw:(g + 1) * gw] = jnp.where(valid & (cnt < MOBA_TOPK), 1.0, 0.0)
            s = s + causal_bias
            m = jnp.max(s, axis=0, keepdims=True)
            p = jnp.exp2(s - m)
            m_scr[g] = m
            l_scr[g] = jnp.sum(p, axis=0, keepdims=True)
            acc_scr[g] = _mm(v_group(g), p, False)

    @pl.when(jnp.logical_not(is_diag))
    def _():
        s_next = scores(0)
        for g in range(N_KV_HEADS):
            s = s_next
            if g + 1 < N_KV_HEADS:
                s_next = scores(g + 1)
            sel = sel_scr[pl.ds(j, 1), g * gw:(g + 1) * gw] > 0.5
            m_prev = m_scr[g]
            m_new = jnp.where(sel, jnp.maximum(m_prev, jnp.max(s, axis=0, keepdims=True)), m_prev)
            alpha = jnp.exp2(m_prev - m_new)
            p = jnp.exp2(s - jnp.where(sel, m_new, jnp.inf))
            l_scr[g] = alpha * l_scr[g] + jnp.sum(p, axis=0, keepdims=True)
            m_scr[g] = m_new
            acc_scr[g] = alpha * acc_scr[g] + _mm(v_group(g), p, False)

    @pl.when(is_last)
    def _():
        tiles = []
        for g in range(N_KV_HEADS):
            og = acc_scr[g] / l_scr[g]
            tiles += [og[:, r * blk:(r + 1) * blk] for r in range(Q_PER_KV)]
        o_ref[0] = jnp.concatenate(tiles, axis=0).T.astype(o_ref.dtype)


def _moba_prompt(qx, kt, vtc):
    B, nb, _, _ = qx.shape
    blk = MOBA_BLOCK
    S = nb * blk
    gw = Q_PER_KV * blk
    precise = False
    nb_pad = -(-nb // SUBLANES) * SUBLANES
    assert nb_pad <= LANES
    it, jt = [], []
    for i in range(nb):
        it += [i] * (i + 1)
        jt += [i] + list(range(i))
    it = jnp.asarray(np.array(it, np.int32))
    jt = jnp.asarray(np.array(jt, np.int32))
    n_steps = int(it.shape[0])
    cdt = F32 if precise else BF16
    grid_spec = pltpu.PrefetchScalarGridSpec(
        num_scalar_prefetch=2,
        grid=(B, n_steps),
        in_specs=[
            pl.BlockSpec((1, 1, HEAD_DIM, N_HEADS * blk), lambda b, t, it, jt: (b, it[t], 0, 0)),
            pl.BlockSpec((1, KV_WIDTH, blk), lambda b, t, it, jt: (b, 0, jt[t])),
            pl.BlockSpec((1, KV_WIDTH, blk), lambda b, t, it, jt: (b, 0, jt[t])),
        ],
        out_specs=pl.BlockSpec((1, blk, ATT_WIDTH), lambda b, t, it, jt: (b, it[t], 0)),
        scratch_shapes=[
            pltpu.VMEM((N_KV_HEADS, HEAD_DIM, gw), F32),
            pltpu.VMEM((N_KV_HEADS, 1, gw), F32),
            pltpu.VMEM((N_KV_HEADS, 1, gw), F32),
            pltpu.VMEM((nb_pad, N_KV_HEADS * gw), F32),
            pltpu.VMEM((KV_WIDTH, LANES), F32),
        ],
    )
    blkb = (_nbytes((HEAD_DIM, N_HEADS * blk), cdt) + _nbytes((KV_WIDTH, blk), F32) + _nbytes((KV_WIDTH, blk), cdt)
            + _nbytes((blk, ATT_WIDTH), cdt))
    scr = (_nbytes((N_KV_HEADS, HEAD_DIM, gw), F32) + 2 * _nbytes((N_KV_HEADS, SUBLANES, gw), F32)
           + _nbytes((nb_pad, N_KV_HEADS * gw), F32) + _nbytes((KV_WIDTH, LANES), F32))
    tmp = 6 * _nbytes((blk, gw), F32)
    return pl.pallas_call(
        functools.partial(_moba_prompt_kernel, nb_pad=nb_pad),
        grid_spec=grid_spec,
        out_shape=jax.ShapeDtypeStruct((B, S, ATT_WIDTH), cdt),
        compiler_params=pltpu.CompilerParams(
            dimension_semantics=("parallel", "arbitrary"),
            vmem_limit_bytes=_vmem_limit(blkb, scr, tmp)),
        name="moba_prompt",
    )(it, jt, qx, kt, vtc)


PAGES_PER_BLOCK = MOBA_BLOCK // PAGE_SIZE
SAMPLE_BLOCKS_PER_STEP = 8


def _moba_sample_kernel(pt_ref, q_ref, *rest, n_blocks, n_steps, dec_seq):
    del pt_ref
    npg = SAMPLE_BLOCKS_PER_STEP * PAGES_PER_BLOCK
    k_pages = rest[:npg]
    v_pages = rest[npg:2 * npg]
    knew_ref, vnew_ref, o_ref, m_scr, l_scr, g_scr, o_scr = rest[2 * npg:]
    s_id = pl.program_id(1)
    qbd = q_ref[0]
    rows = qbd.shape[0]

    for kk in range(SAMPLE_BLOCKS_PER_STEP):
        n = s_id * SAMPLE_BLOCKS_PER_STEP + kk
        ktb = jnp.concatenate([k_pages[kk * PAGES_PER_BLOCK + p][...] for p in range(PAGES_PER_BLOCK)], axis=1)
        vtb = jnp.concatenate([v_pages[kk * PAGES_PER_BLOCK + p][...] for p in range(PAGES_PER_BLOCK)], axis=1)
        sc = _mm(qbd, ktb, False)
        g_scr[n] = jnp.mean(sc, axis=1, keepdims=True)
        m = jnp.max(sc, axis=1, keepdims=True)
        p = jnp.exp(sc - m)
        m_scr[n] = m
        l_scr[n] = jnp.sum(p, axis=1, keepdims=True)
        o_scr[n] = _mm_nt(p, vtb, False)

    @pl.when(s_id == n_steps - 1)
    def _():
        kn = knew_ref[0]
        vn = vnew_ref[0]
        qidx = lax.broadcasted_iota(jnp.int32, (rows, 1), 0) % dec_seq
        cols = []
        for tt in range(dec_seq):
            c = jnp.sum(qbd * kn[tt:tt + 1, :], axis=1, keepdims=True)
            cols.append(jnp.where(tt <= qidx, c, NEG_INF))
        m_own = cols[0]
        for c in cols[1:]:
            m_own = jnp.maximum(m_own, c)
        l_own = jnp.zeros((rows, 1), F32)
        o_own = jnp.zeros((rows, KV_WIDTH), F32)
        for tt in range(dec_seq):
            pt = jnp.exp(cols[tt] - m_own)
            l_own = l_own + pt
            o_own = o_own + pt * vn[tt:tt + 1, :]
        lane = lax.broadcasted_iota(jnp.int32, (rows, LANES), 1)
        gmat = jnp.zeros((rows, LANES), F32)
        mmat = jnp.zeros((rows, LANES), F32)
        lmat = jnp.zeros((rows, LANES), F32)
        for n in range(n_blocks):
            gmat = jnp.where(lane == n, g_scr[n], gmat)
            mmat = jnp.where(lane == n, m_scr[n], mmat)
            lmat = jnp.where(lane == n, l_scr[n], lmat)
        cnt = jnp.zeros((rows, LANES), jnp.int32)
        for n2 in range(n_blocks):
            col = g_scr[n2]
            beats = (col > gmat) | ((col == gmat) & (n2 < lane))
            cnt = cnt + jnp.where(beats, 1, 0)
        sel = (lane < n_blocks) & (cnt < MOBA_TOPK)
        m_all = jnp.maximum(m_own, jnp.max(jnp.where(sel, mmat, NEG_INF), axis=1, keepdims=True))
        wmat = jnp.where(sel, jnp.exp(mmat - m_all), 0.0)
        w_own = jnp.exp(m_own - m_all)
        denom = w_own * l_own + jnp.sum(wmat * lmat, axis=1, keepdims=True)
        out = w_own * o_own
        for n in range(n_blocks):
            out = out + wmat[:, n:n + 1] * o_scr[n]
        o_ref[0] = out / denom


def _moba_sample(qbd, cache_k, cache_v, page_table, k_new, v_new):
    Bd, rows, _ = qbd.shape
    dec_seq = k_new.shape[1]
    n_pages = page_table.shape[1]
    assert (n_pages * PAGE_SIZE) % MOBA_BLOCK == 0
    n_blocks = n_pages // PAGES_PER_BLOCK
    assert n_blocks % SAMPLE_BLOCKS_PER_STEP == 0 and n_blocks <= LANES
    n_steps = n_blocks // SAMPLE_BLOCKS_PER_STEP
    npg = SAMPLE_BLOCKS_PER_STEP * PAGES_PER_BLOCK

    def page_spec(p):
        return pl.BlockSpec((None, KV_WIDTH, PAGE_SIZE), lambda b, s, pt: (pt[b, s * npg + p], 0, 0))

    grid_spec = pltpu.PrefetchScalarGridSpec(
        num_scalar_prefetch=1,
        grid=(Bd, n_steps),
        in_specs=([pl.BlockSpec((1, rows, KV_WIDTH), lambda b, s, pt: (b, 0, 0))]
                  + [page_spec(p) for p in range(npg)] + [page_spec(p) for p in range(npg)]
                  + [pl.BlockSpec((1, dec_seq, KV_WIDTH), lambda b, s, pt: (b, 0, 0))] * 2),
        out_specs=pl.BlockSpec((1, rows, KV_WIDTH), lambda b, s, pt: (b, 0, 0)),
        scratch_shapes=[
            pltpu.VMEM((n_blocks, rows, 1), F32),
            pltpu.VMEM((n_blocks, rows, 1), F32),
            pltpu.VMEM((n_blocks, rows, 1), F32),
            pltpu.VMEM((n_blocks, rows, KV_WIDTH), F32),
        ],
    )
    blkb = (2 * _nbytes((rows, KV_WIDTH), F32) + 2 * npg * _nbytes((PAGE_SIZE, KV_WIDTH), F32)
            + 2 * _nbytes((SUBLANES, KV_WIDTH), F32))
    scr = 3 * _nbytes((n_blocks, rows, LANES), F32) + _nbytes((n_blocks, rows, KV_WIDTH), F32)
    tmp = 8 * _nbytes((MOBA_BLOCK, KV_WIDTH), F32)
    return pl.pallas_call(
        functools.partial(_moba_sample_kernel, n_blocks=n_blocks, n_steps=n_steps, dec_seq=dec_seq),
        grid_spec=grid_spec,
        out_shape=jax.ShapeDtypeStruct((Bd, rows, KV_WIDTH), F32),
        compiler_params=pltpu.CompilerParams(
            dimension_semantics=("parallel", "arbitrary"),
            vmem_limit_bytes=_vmem_limit(blkb, scr, tmp)),
        name="moba_sample",
    )(page_table, qbd, *([cache_k] * npg), *([cache_v] * npg), k_new, v_new)


CONV_PAD = SUBLANES


def _ssd_kernel(*refs, precise, n_valid, n_chunks, has_state, n_ssm_heads, d_inner):
    tc = SSD_CHUNK
    refs = list(refs)
    z_ref, xbc_ref, dt_ref = refs[:3]
    del refs[:3]
    if has_state:
        h0_ref, pre_ref = refs[:2]
        del refs[:2]
    (cw_ref, cb_ref, dtb_ref, a_ref, dsk_ref, nw_ref, o_ref, st_ref, cv_ref,
     ext_scr, h_scr, xt_scr, xd_scr, yt_scr) = refs[:14]
    if n_valid < tc:
        dtp_scr, zp_scr = refs[14:]
    c = pl.program_id(1)
    gw = SSM_GROUPS * D_STATE
    heads_per_group = n_ssm_heads // SSM_GROUPS
    rows_per_group = heads_per_group * SSM_HEAD_DIM

    @pl.when(c == 0)
    def _():
        ext_scr[...] = jnp.zeros_like(ext_scr)
        if has_state:
            ext_scr[CONV_PAD - (CONV_WIDTH - 1):CONV_PAD, :] = pre_ref[0]
            h_scr[...] = h0_ref[0]
        else:
            h_scr[...] = jnp.zeros_like(h_scr)

    ext_scr[CONV_PAD:CONV_PAD + n_valid, :] = xbc_ref[0]
    conv = cb_ref[...]
    for w in range(CONV_WIDTH):
        off = CONV_PAD - (CONV_WIDTH - 1) + w
        conv = conv + ext_scr[off:off + tc, :] * cw_ref[w:w + 1, :]
    xc = _silu(conv)

    @pl.when(c == n_chunks - 1)
    def _():
        lo = CONV_PAD + n_valid - (CONV_WIDTH - 1)
        cv_ref[0] = ext_scr[lo:lo + CONV_WIDTH - 1, :]

    ext_scr[CONV_PAD - (CONV_WIDTH - 1):CONV_PAD, :] = ext_scr[CONV_PAD + tc - (CONV_WIDTH - 1):CONV_PAD + tc, :]

    if n_valid == tc:
        dt_raw = dt_ref[0]
    else:
        dtp_scr[...] = jnp.zeros_like(dtp_scr)
        dtp_scr[0:n_valid, :] = dt_ref[0]
        dt_raw = dtp_scr[...]
    xdt_in = dt_raw + dtb_ref[...]
    dt = jnp.maximum(xdt_in, 0.0) + jnp.log1p(jnp.exp(-jnp.abs(xdt_in)))
    t_iota = lax.broadcasted_iota(jnp.int32, (tc, LANES), 0)
    dt = jnp.where(t_iota < n_valid, dt, 0.0)
    d_a = dt * a_ref[...]
    tri = (lax.broadcasted_iota(jnp.int32, (tc, tc), 0) >= lax.broadcasted_iota(jnp.int32, (tc, tc), 1))
    acum = jnp.dot(tri.astype(F32), d_a, precision=lax.Precision.HIGHEST, preferred_element_type=F32)
    dt_t = dt.T
    acum_t = acum.T
    a_last = acum_t[:, tc - 1:tc]
    exp_a_t = jnp.exp(acum_t)
    dec_t = jnp.exp(a_last - acum_t)
    chunk_dec = jnp.broadcast_to(jnp.exp(a_last), (LANES, D_STATE))

    xt_scr[...] = xc[:, :d_inner].T
    s_iota = lax.broadcasted_iota(jnp.int32, (tc, tc), 0)
    u_iota = lax.broadcasted_iota(jnp.int32, (tc, tc), 1)
    causal = s_iota <= u_iota
    for g in range(SSM_GROUPS):
        bg = xc[:, d_inner + g * D_STATE:d_inner + (g + 1) * D_STATE]
        cg = xc[:, d_inner + gw + g * D_STATE:d_inner + gw + (g + 1) * D_STATE]
        cbt = _mm_nt(bg, cg, precise)
        rs = slice(g * rows_per_group, (g + 1) * rows_per_group)
        hg = h_scr[rs, :]
        yoff = _mm_nt(hg, cg, precise)
        for hh in range(heads_per_group):
            h = g * heads_per_group + hh
            hs = slice(h * SSM_HEAD_DIM, (h + 1) * SSM_HEAD_DIM)
            xh = xt_scr[hs, :]
            xdt = xh * dt_t[h:h + 1, :]
            seg = acum_t[h:h + 1, :] - acum[:, h:h + 1]
            lmat = jnp.exp(jnp.where(causal, seg, NEG_INF))
            yd = _mm(xdt, cbt * lmat, precise)
            yt_scr[hs, :] = (yd + yoff[hh * SSM_HEAD_DIM:(hh + 1) * SSM_HEAD_DIM, :] * exp_a_t[h:h + 1, :]
                             + dsk_ref[h:h + 1, :] * xh)
            xd_scr[hs, :] = xdt * dec_t[h:h + 1, :]
        st = _mm(xd_scr[rs, :], bg, precise)
        for hh in range(heads_per_group):
            h = g * heads_per_group + hh
            hs = slice(h * SSM_HEAD_DIM, (h + 1) * SSM_HEAD_DIM)
            lo = hh * SSM_HEAD_DIM
            h_scr[hs, :] = hg[lo:lo + SSM_HEAD_DIM, :] * chunk_dec[h:h + 1, :] + st[lo:lo + SSM_HEAD_DIM, :]

    @pl.when(c == n_chunks - 1)
    def _():
        st_ref[0] = h_scr[...]

    y = yt_scr[...].T
    if n_valid == tc:
        zz = z_ref[0]
    else:
        zp_scr[...] = jnp.zeros_like(zp_scr)
        zp_scr[0:n_valid, :] = z_ref[0]
        zz = zp_scr[...]
    yz = y * _silu(zz)
    gsz = d_inner // SSM_GROUPS
    outs = []
    for g in range(SSM_GROUPS):
        grp = yz[:, g * gsz:(g + 1) * gsz]
        outs.append(grp * lax.rsqrt(jnp.mean(grp * grp, axis=-1, keepdims=True) + RMS_EPS))
    res = jnp.concatenate(outs, axis=1) * nw_ref[...]
    o_ref[0] = res[:n_valid, :].astype(o_ref.dtype)


def _ssd(z3, xbc3, dt3, h0, prefix, conv_w, conv_b, dt_bias, a_log, d_skip, ssm_norm_w, *, precise):
    B, S, d_inner = z3.shape
    conv_ch = xbc3.shape[2]
    n_heads = d_inner // SSM_HEAD_DIM
    tc = SSD_CHUNK
    if S % tc == 0:
        n_valid, n_chunks = tc, S // tc
    else:
        assert S < tc
        n_valid, n_chunks = S, 1
    has_state = h0 is not None
    cdt = F32 if precise else BF16
    pad_h = LANES - n_heads
    dtb = jnp.pad(dt_bias.astype(F32), (0, pad_h)).reshape(1, LANES)
    a_row = jnp.pad(-jnp.exp(a_log.astype(F32)), (0, pad_h)).reshape(1, LANES)
    dsk = jnp.broadcast_to(jnp.pad(d_skip.astype(F32), (0, pad_h))[:, None], (LANES, tc))

    full2 = lambda shp: pl.BlockSpec(shp, lambda b, c: (0,) * len(shp))
    in_specs = [
        pl.BlockSpec((1, n_valid, d_inner), lambda b, c: (b, c, 0)),
        pl.BlockSpec((1, n_valid, conv_ch), lambda b, c: (b, c, 0)),
        pl.BlockSpec((1, n_valid, LANES), lambda b, c: (b, c, 0)),
    ]
    args = [z3, xbc3, dt3]
    if has_state:
        in_specs += [pl.BlockSpec((1, d_inner, D_STATE), lambda b, c: (b, 0, 0)),
                     pl.BlockSpec((1, CONV_WIDTH - 1, conv_ch), lambda b, c: (b, 0, 0))]
        args += [h0, prefix]
    in_specs += [full2((CONV_WIDTH, conv_ch)), full2((1, conv_ch)), full2((1, LANES)), full2((1, LANES)),
                 full2((LANES, tc)), full2((1, d_inner))]
    args += [conv_w, conv_b.reshape(1, conv_ch), dtb, a_row, dsk, ssm_norm_w.reshape(1, d_inner)]
    out_shape = [jax.ShapeDtypeStruct((B, S, d_inner), cdt),
                 jax.ShapeDtypeStruct((B, d_inner, D_STATE), F32),
                 jax.ShapeDtypeStruct((B, CONV_WIDTH - 1, conv_ch), F32)]
    out_specs = [pl.BlockSpec((1, n_valid, d_inner), lambda b, c: (b, c, 0)),
                 pl.BlockSpec((1, d_inner, D_STATE), lambda b, c: (b, 0, 0)),
                 pl.BlockSpec((1, CONV_WIDTH - 1, conv_ch), lambda b, c: (b, 0, 0))]
    scratch = [pltpu.VMEM((CONV_PAD + tc, conv_ch), F32), pltpu.VMEM((d_inner, D_STATE), F32),
               pltpu.VMEM((d_inner, tc), F32), pltpu.VMEM((d_inner, tc), F32), pltpu.VMEM((d_inner, tc), F32)]
    if n_valid < tc:
        scratch += [pltpu.VMEM((tc, LANES), F32), pltpu.VMEM((tc, d_inner), F32)]
    blk = (_nbytes((tc, d_inner), F32) + _nbytes((tc, conv_ch), F32) + _nbytes((tc, LANES), F32)
           + 2 * _nbytes((d_inner, D_STATE), F32) + 2 * _nbytes((SUBLANES, conv_ch), F32)
           + _nbytes((CONV_WIDTH + SUBLANES, conv_ch), F32) + _nbytes((tc, d_inner), cdt))
    scr = _nbytes((CONV_PAD + tc, conv_ch), F32) + 4 * _nbytes((d_inner, D_STATE), F32)
    tmp = 3 * _nbytes((tc, conv_ch), F32) + 4 * _nbytes((tc, d_inner), F32)
    return pl.pallas_call(
        functools.partial(_ssd_kernel, precise=precise, n_valid=n_valid, n_chunks=n_chunks,
                          has_state=has_state, n_ssm_heads=n_heads, d_inner=d_inner),
        grid=(B, n_chunks),
        in_specs=in_specs,
        out_specs=out_specs,
        out_shape=out_shape,
        scratch_shapes=scratch,
        compiler_params=pltpu.CompilerParams(
            dimension_semantics=("parallel", "arbitrary"),
            vmem_limit_bytes=_vmem_limit(blk, scr, tmp)),
        name="ssd",
    )(*args)


def _merge_kernel(x_ref, att_ref, ssd_ref, ga_ref, gb_ref, wpa_ref, wpb_ref, wo_ref, o_ref, *, precise):
    a = _mm(att_ref[...], wpa_ref[...], precise)
    b = _mm(ssd_ref[...], wpb_ref[...], precise)
    m = _sigmoid(ga_ref[...]) * a + _sigmoid(gb_ref[...]) * b
    o_ref[...] = x_ref[...] + _mm(m, wo_ref[...], precise)


def _merge(x2d, att, ssd, ga, gb, w_pa, w_pb, w_o, *, precise):
    T, D = x2d.shape
    d_inner = ssd.shape[1]
    tm = min(512, T)
    assert T % tm == 0
    cdt = F32 if precise else BF16
    row = lambda w: pl.BlockSpec((tm, w), lambda i: (i, 0))
    full = lambda r, w: pl.BlockSpec((r, w), lambda i: (0, 0))
    blk = (4 * _nbytes((tm, D), F32) + _nbytes((tm, ATT_WIDTH), cdt) + _nbytes((tm, d_inner), cdt)
           + _nbytes((ATT_WIDTH + d_inner + D, D), cdt))
    tmp = 4 * _nbytes((tm, D), F32)
    return pl.pallas_call(
        functools.partial(_merge_kernel, precise=precise),
        grid=(T // tm,),
        in_specs=[row(D), row(ATT_WIDTH), row(d_inner), row(D), row(D),
                  full(ATT_WIDTH, D), full(d_inner, D), full(D, D)],
        out_specs=row(D),
        out_shape=jax.ShapeDtypeStruct((T, D), F32),
        compiler_params=pltpu.CompilerParams(
            dimension_semantics=("parallel",),
            vmem_limit_bytes=_vmem_limit(blk, 0, tmp)),
        name="merge",
    )(x2d, att, ssd, ga, gb, w_pa, w_pb, w_o)


def _moe_kernel(x_ref, n2_ref, wr_ref, wg_ref, wu_ref, wd_ref, nf_ref, o_ref, h_scr, gate_scr, acc_scr,
                *, precise, final_norm):
    e = pl.program_id(1)
    tm = x_ref.shape[0]
    epg = EXPERTS_PER_GROUP

    @pl.when(e == 0)
    def _():
        x = x_ref[...]
        h = x * lax.rsqrt(jnp.mean(x * x, axis=-1, keepdims=True) + RMS_EPS) * n2_ref[...]
        h_scr[...] = h.astype(h_scr.dtype)
        acc_scr[...] = jnp.zeros_like(acc_scr)
        lg = _mm(h, wr_ref[...], precise)
        gl = [lg[:, k:k + 1] for k in range(N_EXPERT_GROUPS)]
        gmax = functools.reduce(jnp.maximum, gl)
        taken = jnp.zeros((tm, 1), jnp.bool_)
        is_g = []
        for k in range(N_EXPERT_GROUPS):
            hit = (gl[k] == gmax) & (~taken)
            is_g.append(hit)
            taken = taken | hit
        p_grp = 1.0 / functools.reduce(lambda a, b: a + b, [jnp.exp(v - gmax) for v in gl])
        le = []
        for k in range(epg):
            v = jnp.zeros((tm, 1), F32)
            for gi in range(N_EXPERT_GROUPS):
                c0 = N_EXPERT_GROUPS + gi * epg + k
                v = jnp.where(is_g[gi], lg[:, c0:c0 + 1], v)
            le.append(v)

        def first_max(vals):
            mx = functools.reduce(jnp.maximum, vals)
            tk = jnp.zeros((tm, 1), jnp.bool_)
            hits = []
            for v in vals:
                hit = (v == mx) & (~tk)
                hits.append(hit)
                tk = tk | hit
            return mx, hits

        v1, o1 = first_max(le)
        v2, o2 = first_max([jnp.where(o1[k], NEG_INF, le[k]) for k in range(epg)])
        e21 = jnp.exp(v2 - v1)
        w1 = p_grp / (1.0 + e21)
        w2 = p_grp * e21 / (1.0 + e21)
        lane = lax.broadcasted_iota(jnp.int32, (tm, LANES), 1)
        for gi in range(N_EXPERT_GROUPS):
            gm = jnp.zeros((tm, LANES), F32)
            for k in range(epg):
                wk = jnp.where(o1[k], w1, jnp.where(o2[k], w2, 0.0))
                gm = jnp.where(lane == k, jnp.where(is_g[gi], wk, 0.0), gm)
            gate_scr[gi] = gm

    h = h_scr[...]
    gate = gate_scr[e]
    cdt = F32 if precise else BF16
    acc = acc_scr[...]
    for k in range(epg):
        hg = _mm(h, wg_ref[k], precise)
        hu = _mm(h, wu_ref[k], precise)
        hid = (_silu(hg) * hu * gate[:, k:k + 1]).astype(cdt)
        acc = acc + _mm(hid, wd_ref[k], precise)
    acc_scr[...] = acc

    @pl.when(e == N_EXPERT_GROUPS - 1)
    def _():
        y = x_ref[...] + acc_scr[...]
        if final_norm:
            y = y * lax.rsqrt(jnp.mean(y * y, axis=-1, keepdims=True) + RMS_EPS) * nf_ref[...]
        o_ref[...] = y


def _moe(x2d, norm2_w, w_router, w_g, w_u, w_d, norm_f_w, *, precise, final_norm):
    T, D = x2d.shape
    n_e, _, d_e = w_g.shape
    assert n_e == N_EXPERTS
    tm = min(1024, T)
    assert T % tm == 0
    cdt = F32 if precise else BF16
    epg = EXPERTS_PER_GROUP
    blk = (2 * _nbytes((tm, D), F32) + 2 * _nbytes((1, D), F32) + _nbytes((D, LANES), F32)
           + 3 * _nbytes((epg, D, d_e), cdt))
    scr = _nbytes((tm, D), cdt) + _nbytes((N_EXPERT_GROUPS, tm, LANES), F32) + _nbytes((tm, D), F32)
    tmp = 6 * _nbytes((tm, d_e), F32) + 2 * _nbytes((tm, D), F32) + 64 * _nbytes((tm, LANES), F32)
    return pl.pallas_call(
        functools.partial(_moe_kernel, precise=precise, final_norm=final_norm),
        grid=(T // tm, N_EXPERT_GROUPS),
        in_specs=[
            pl.BlockSpec((tm, D), lambda i, e: (i, 0)),
            pl.BlockSpec((1, D), lambda i, e: (0, 0)),
            pl.BlockSpec((D, LANES), lambda i, e: (0, 0)),
            pl.BlockSpec((epg, D, d_e), lambda i, e: (e, 0, 0)),
            pl.BlockSpec((epg, D, d_e), lambda i, e: (e, 0, 0)),
            pl.BlockSpec((epg, d_e, D), lambda i, e: (e, 0, 0)),
            pl.BlockSpec((1, D), lambda i, e: (0, 0)),
        ],
        out_specs=pl.BlockSpec((tm, D), lambda i, e: (i, 0)),
        out_shape=jax.ShapeDtypeStruct((T, D), F32),
        scratch_shapes=[pltpu.VMEM((tm, D), cdt), pltpu.VMEM((N_EXPERT_GROUPS, tm, LANES), F32),
                        pltpu.VMEM((tm, D), F32)],
        compiler_params=pltpu.CompilerParams(
            dimension_semantics=("parallel", "arbitrary"),
            vmem_limit_bytes=_vmem_limit(blk, scr, tmp)),
        name="moe",
    )(x2d, norm2_w.reshape(1, D), w_router, w_g, w_u, w_d, norm_f_w.reshape(1, D))


def _rope_tables(pos):
    inv_freq = 1.0 / (ROPE_THETA ** (jnp.arange(0, HEAD_DIM, 2, dtype=F32) / HEAD_DIM))
    ang = pos.astype(F32)[:, None] * inv_freq[None, :]
    ang = jnp.concatenate([ang, ang], axis=-1)
    sign = jnp.concatenate([-jnp.ones((HEAD_DIM // 2,), F32), jnp.ones((HEAD_DIM // 2,), F32)])
    reps = LANES // HEAD_DIM
    return jnp.tile(jnp.cos(ang), (1, reps)), jnp.tile(jnp.sin(ang) * sign, (1, reps))


def _layer_weights(lw, precise):
    (norm1_w, w_in, conv_w, conv_b, dt_bias, a_log, d_skip, ssm_norm_w,
     w_pa, w_pb, w_o, norm2_w, w_rg, w_re, w_g, w_u, w_d) = lw
    cdt = F32 if precise else BF16
    D = w_in.shape[0]
    d_inner = ssm_norm_w.shape[0]
    conv_ch = conv_w.shape[1]
    n_heads = dt_bias.shape[0]
    dt0 = ATT_WIDTH + 2 * KV_WIDTH + d_inner + conv_ch
    w_main = jnp.concatenate([w_in[:, :dt0], w_in[:, dt0 + n_heads:]], axis=1).astype(cdt)
    w_dt = jnp.pad(w_in[:, dt0:dt0 + n_heads], ((0, 0), (0, LANES - n_heads))).astype(cdt)
    w_router = jnp.pad(jnp.concatenate([w_rg, w_re], axis=1),
                       ((0, 0), (0, LANES - N_EXPERT_GROUPS - N_EXPERTS))).astype(F32)
    return dict(norm1_w=norm1_w, w_main=w_main, w_dt=w_dt, conv_w=conv_w, conv_b=conv_b, dt_bias=dt_bias,
                a_log=a_log, d_skip=d_skip, ssm_norm_w=ssm_norm_w, w_pa=w_pa.astype(cdt),
                w_pb=w_pb.astype(cdt), w_o=w_o.astype(cdt), norm2_w=norm2_w, w_router=w_router,
                w_g=w_g.astype(cdt), w_u=w_u.astype(cdt), w_d=w_d.astype(cdt),
                d_inner=d_inner, conv_ch=conv_ch)


def _pages_transposed(cache):
    n_phys = cache.shape[0]
    return cache.transpose(0, 2, 3, 1).reshape(n_phys, KV_WIDTH, PAGE_SIZE)


def _prompt_layer(x3, W, norm_f_w, final_norm):
    B, S, D = x3.shape
    T = B * S
    cos_t, sin_t = _rope_tables(jnp.arange(S, dtype=jnp.int32))
    qx, kt, vt, vtc, z, xbc, ga, gb, dt = _inproj(
        x3.reshape(T, D), B, W["norm1_w"], W["w_main"], W["w_dt"], cos_t, sin_t,
        precise=False, d_inner=W["d_inner"], conv_ch=W["conv_ch"], q_scale=ATT_SCALE * LOG2E, q_blocks=True)
    att = _moba_prompt(qx, kt, vtc)
    k = kt.reshape(B, N_KV_HEADS, HEAD_DIM, S).transpose(0, 3, 1, 2)
    v = vt.reshape(B, N_KV_HEADS, HEAD_DIM, S).transpose(0, 3, 1, 2)
    ssd, state, conv_new = _ssd(
        z.reshape(B, S, -1), xbc.reshape(B, S, -1), dt.reshape(B, S, LANES), None, None,
        W["conv_w"], W["conv_b"], W["dt_bias"], W["a_log"], W["d_skip"], W["ssm_norm_w"], precise=False)
    x1 = _merge(x3.reshape(T, D), att.reshape(T, ATT_WIDTH), ssd.reshape(T, -1), ga, gb,
                W["w_pa"], W["w_pb"], W["w_o"], precise=False)
    y = _moe(x1, W["norm2_w"], W["w_router"], W["w_g"], W["w_u"], W["w_d"], norm_f_w,
             precise=False, final_norm=final_norm)
    return y.reshape(B, S, D), k, v, state, conv_new


def _sample_layer(x3, W, norm_f_w, final_norm, cache_k, cache_v, page_table, h0, prefix):
    Bd, Sd, D = x3.shape
    T = Bd * Sd
    past_len = page_table.shape[1] * PAGE_SIZE
    pos = past_len + (jnp.arange(T, dtype=jnp.int32) % Sd)
    cos_t, sin_t = _rope_tables(pos)
    q, kt, vt, _, z, xbc, ga, gb, dt = _inproj(
        x3.reshape(T, D), 1, W["norm1_w"], W["w_main"], W["w_dt"], cos_t, sin_t,
        precise=True, d_inner=W["d_inner"], conv_ch=W["conv_ch"], q_scale=ATT_SCALE, q_blocks=False)
    k = kt[0].T
    v = vt[0].T
    q = q.reshape(Bd, Sd, N_HEADS, HEAD_DIM).transpose(0, 2, 1, 3)
    grp_of_head = jnp.arange(N_HEADS) // Q_PER_KV
    onehot = (grp_of_head[:, None] == jnp.arange(N_KV_HEADS)[None, :]).astype(F32)
    qbd = (q[:, :, :, None, :] * onehot[None, :, None, :, None]).reshape(Bd, N_HEADS * Sd, KV_WIDTH)
    o = _moba_sample(qbd, cache_k, cache_v, page_table,
                     k.reshape(Bd, Sd, KV_WIDTH), v.reshape(Bd, Sd, KV_WIDTH))
    o = o.reshape(Bd, N_HEADS, Sd, N_KV_HEADS, HEAD_DIM)
    o = jnp.sum(o * onehot[None, :, None, :, None], axis=3)
    att = o.transpose(0, 2, 1, 3).reshape(T, ATT_WIDTH)
    ssd, state, conv_new = _ssd(
        z.reshape(Bd, Sd, -1), xbc.reshape(Bd, Sd, -1), dt.reshape(Bd, Sd, LANES), h0, prefix,
        W["conv_w"], W["conv_b"], W["dt_bias"], W["a_log"], W["d_skip"], W["ssm_norm_w"], precise=True)
    x1 = _merge(x3.reshape(T, D), att, ssd.reshape(T, -1), ga, gb,
                W["w_pa"], W["w_pb"], W["w_o"], precise=True)
    y = _moe(x1, W["norm2_w"], W["w_router"], W["w_g"], W["w_u"], W["w_d"], norm_f_w,
             precise=True, final_norm=final_norm)
    return y.reshape(Bd, Sd, D), k, v, state, conv_new


def kernel(x_prompt, x_sample, cache_k, cache_v, state_ssm, state_conv, page_table, norm1_w, w_in, conv_w,
           conv_b, dt_bias, a_log, d_skip, ssm_norm_w, w_pa, w_pb, w_o, norm2_w, w_router_group,
           w_router_expert, w_gate_e, w_up_e, w_down_e, norm_f_w):
    depth = w_in.shape[0]
    B, S, D = x_prompt.shape
    Bd, Sd, _ = x_sample.shape
    n_phys = cache_k.shape[1]
    hp, hs = x_prompt, x_sample
    outs = [[] for _ in range(8)]
    for l in range(depth):
        lw = (norm1_w[l], w_in[l], conv_w[l], conv_b[l], dt_bias[l], a_log[l], d_skip[l], ssm_norm_w[l],
              w_pa[l], w_pb[l], w_o[l], norm2_w[l], w_router_group[l], w_router_expert[l],
              w_gate_e[l], w_up_e[l], w_down_e[l])
        last = l == depth - 1
        hp, kp, vp, sp, cp = _prompt_layer(hp, _layer_weights(lw, False), norm_f_w, last)
        n_ssm_heads, ssm_hd, d_state = state_ssm.shape[2:]
        hs, ks, vs, ss, cs = _sample_layer(
            hs, _layer_weights(lw, True), norm_f_w, last,
            _pages_transposed(cache_k[l]), _pages_transposed(cache_v[l]),
            page_table, state_ssm[l].reshape(Bd, n_ssm_heads * ssm_hd, d_state), state_conv[l])
        outs[0].append(kp)
        outs[1].append(vp)
        outs[2].append(sp.reshape(B, n_ssm_heads, ssm_hd, d_state))
        outs[3].append(cp)
        outs[4].append(ks.reshape(Bd, Sd, N_KV_HEADS, HEAD_DIM))
        outs[5].append(vs.reshape(Bd, Sd, N_KV_HEADS, HEAD_DIM))
        outs[6].append(ss.reshape(Bd, n_ssm_heads, ssm_hd, d_state))
        outs[7].append(cs)
    return (hp, hs) + tuple(jnp.stack(o) for o in outs)
```

```python
import functools
import math

import numpy as np
import jax
import jax.numpy as jnp
from jax import lax
from jax.experimental import pallas as pl
from jax.experimental.pallas import tpu as pltpu

N_HEADS = 16
HEAD_DIM = 64
N_KV_HEADS = 4
Q_PER_KV = N_HEADS // N_KV_HEADS
ATT_WIDTH = N_HEADS * HEAD_DIM
KV_WIDTH = N_KV_HEADS * HEAD_DIM
MOBA_BLOCK = 256
MOBA_TOPK = 3
ROPE_THETA = 10000.0
ATT_SCALE = 1.0 / math.sqrt(HEAD_DIM)
LOG2E = math.log2(math.e)
SSM_HEAD_DIM = 64
SSM_GROUPS = 4
D_STATE = 128
CONV_WIDTH = 4
SSD_CHUNK = 128
N_EXPERT_GROUPS = 4
EXPERTS_PER_GROUP = 4
N_EXPERTS = N_EXPERT_GROUPS * EXPERTS_PER_GROUP
RMS_EPS = 1e-6
PAGE_SIZE = 128

LANES = 128
SUBLANES = 8
V7X_VMEM_BYTES = 64 * 1024 * 1024
VMEM_COMPILER_RESERVE = 8 * 1024 * 1024

F32 = jnp.float32
BF16 = jnp.bfloat16
NEG_INF = float("-inf")


def _vmem_limit(block_bytes, scratch_bytes, temp_bytes):
    need = 2 * block_bytes + scratch_bytes + temp_bytes + VMEM_COMPILER_RESERVE
    return int(min(need, V7X_VMEM_BYTES - VMEM_COMPILER_RESERVE))


def _nbytes(shape, dtype):
    return int(np.prod(shape)) * jnp.dtype(dtype).itemsize


def _mm(a, b, precise):
    if precise:
        return jnp.dot(a.astype(F32), b.astype(F32), precision=lax.Precision.HIGHEST,
                       preferred_element_type=F32)
    return jnp.dot(a.astype(BF16), b.astype(BF16), preferred_element_type=F32)


def _mm_nt(a, b, precise):
    dn = (((1,), (1,)), ((), ()))
    if precise:
        return lax.dot_general(a.astype(F32), b.astype(F32), dn, precision=lax.Precision.HIGHEST,
                               preferred_element_type=F32)
    return lax.dot_general(a.astype(BF16), b.astype(BF16), dn, preferred_element_type=F32)


def _mm_tn(a, b):
    dn = (((0,), (0,)), ((), ()))
    return lax.dot_general(a.astype(BF16), b.astype(BF16), dn, preferred_element_type=F32)


def _sigmoid(x):
    return 1.0 / (1.0 + jnp.exp(-x))


def _silu(x):
    return x * _sigmoid(x)


IN_TN = 512
IN_Q_TILES = ATT_WIDTH // IN_TN
IN_KV_TILE = IN_Q_TILES


def _norm_to_scratch(x_ref, nw_ref, h_scr):
    x = x_ref[...]
    h = x * lax.rsqrt(jnp.mean(x * x, axis=-1, keepdims=True) + RMS_EPS)
    h_scr[...] = (h * nw_ref[...]).astype(h_scr.dtype)


def _inproj_main_kernel(x_ref, nw_ref, w_ref, o_ref, h_scr, *, precise):
    @pl.when(pl.program_id(1) == 0)
    def _():
        _norm_to_scratch(x_ref, nw_ref, h_scr)

    o_ref[...] = _mm(h_scr[...], w_ref[...], precise)


def _inproj_qkv_kernel(x_ref, nw_ref, w_ref, wdt_ref, cos_ref, sin_ref, q_ref, kt_ref, vt_ref, dt_ref, *rest,
                       precise, q_scale, q_blocks):
    if q_blocks:
        ktb_ref, vtb_ref, h_scr = rest
    else:
        (h_scr,) = rest
    j = pl.program_id(1)
    tm = x_ref.shape[0]

    @pl.when(j == 0)
    def _():
        _norm_to_scratch(x_ref, nw_ref, h_scr)

    def rope(a):
        cos = cos_ref[...]
        sin = sin_ref[...]
        lane = lax.broadcasted_iota(jnp.int32, (tm, LANES), 1)
        first_half = (lane % HEAD_DIM) < (HEAD_DIM // 2)
        outs = []
        for c in range(a.shape[1] // LANES):
            xc = a[:, c * LANES:(c + 1) * LANES]
            rot = jnp.where(first_half, pltpu.roll(xc, LANES - HEAD_DIM // 2, 1),
                            pltpu.roll(xc, HEAD_DIM // 2, 1))
            outs.append(xc * cos + rot * sin)
        return jnp.concatenate(outs, axis=1)

    @pl.when(j < IN_KV_TILE)
    def _():
        qr = rope(_mm(h_scr[...], w_ref[...], precise)) * q_scale
        if q_blocks:
            qrt = qr.T.astype(q_ref.dtype)
            for hh in range(IN_TN // HEAD_DIM):
                for bi in range(tm // MOBA_BLOCK):
                    q_ref[0, bi, :, hh * MOBA_BLOCK:(hh + 1) * MOBA_BLOCK] = (
                        qrt[hh * HEAD_DIM:(hh + 1) * HEAD_DIM, bi * MOBA_BLOCK:(bi + 1) * MOBA_BLOCK])
        else:
            q_ref[...] = qr.astype(q_ref.dtype)

    @pl.when(j == IN_KV_TILE)
    def _():
        acc = _mm(h_scr[...], w_ref[...], precise)
        kt = rope(acc[:, :KV_WIDTH]).T
        vt = acc[:, KV_WIDTH:].T
        kt_ref[0] = kt
        vt_ref[0] = vt
        if q_blocks:
            for bi in range(tm // MOBA_BLOCK):
                cols = slice(bi * MOBA_BLOCK, (bi + 1) * MOBA_BLOCK)
                ktb_ref[0, bi] = kt[:, cols].astype(ktb_ref.dtype)
                vtb_ref[0, bi] = vt[:, cols].astype(vtb_ref.dtype)
        dt_ref[...] = _mm(h_scr[...], wdt_ref[...], precise)


IN_MAIN_TN = 1024


def _inproj_main(x2d, norm_w, w_rest, *, precise):
    T, D = x2d.shape
    n_rest = w_rest.shape[1]
    tm = min(1024, T)
    assert T % tm == 0 and n_rest % IN_MAIN_TN == 0
    cdt = F32 if precise else BF16
    blk = _nbytes((tm, D), F32) + _nbytes((D, IN_MAIN_TN), cdt) + _nbytes((tm, IN_MAIN_TN), F32)
    return pl.pallas_call(
        functools.partial(_inproj_main_kernel, precise=precise),
        grid=(T // tm, n_rest // IN_MAIN_TN),
        in_specs=[pl.BlockSpec((tm, D), lambda i, j: (i, 0)),
                  pl.BlockSpec((1, D), lambda i, j: (0, 0)),
                  pl.BlockSpec((D, IN_MAIN_TN), lambda i, j: (0, j))],
        out_specs=pl.BlockSpec((tm, IN_MAIN_TN), lambda i, j: (i, j)),
        out_shape=jax.ShapeDtypeStruct((T, n_rest), F32),
        scratch_shapes=[pltpu.VMEM((tm, D), cdt)],
        compiler_params=pltpu.CompilerParams(
            dimension_semantics=("parallel", "arbitrary"),
            vmem_limit_bytes=_vmem_limit(blk, _nbytes((tm, D), cdt), _nbytes((tm, IN_MAIN_TN), F32))),
        name="inproj_main",
    )(x2d, norm_w.reshape(1, D), w_rest)


def _inproj_qkv(x2d, n_batch, norm_w, w_qkv, w_dt, cos_t, sin_t, *, precise, q_scale, q_blocks):
    T, D = x2d.shape
    S = T // n_batch
    tm = min(1024, S)
    assert S % tm == 0 and ATT_WIDTH % IN_TN == 0 and 2 * KV_WIDTH == IN_TN
    assert not q_blocks or tm % MOBA_BLOCK == 0
    tiles_per_b = S // tm
    n_col = IN_KV_TILE + 1
    assert w_qkv.shape == (D, n_col * IN_TN)
    cdt = F32 if precise else BF16

    in_specs = [
        pl.BlockSpec((tm, D), lambda i, j: (i, 0)),
        pl.BlockSpec((1, D), lambda i, j: (0, 0)),
        pl.BlockSpec((D, IN_TN), lambda i, j: (0, j)),
        pl.BlockSpec((D, LANES), lambda i, j: (0, 0)),
        pl.BlockSpec((tm, LANES), lambda i, j: (i % tiles_per_b, 0)),
        pl.BlockSpec((tm, LANES), lambda i, j: (i % tiles_per_b, 0)),
    ]
    jq = lambda j: jnp.clip(j, 0, IN_Q_TILES - 1)
    if q_blocks:
        bpt = tm // MOBA_BLOCK
        q_shape = jax.ShapeDtypeStruct((n_batch, S // MOBA_BLOCK, HEAD_DIM, N_HEADS * MOBA_BLOCK), cdt)
        q_spec = pl.BlockSpec((1, bpt, HEAD_DIM, (IN_TN // HEAD_DIM) * MOBA_BLOCK),
                              lambda i, j: (i // tiles_per_b, i % tiles_per_b, 0, jq(j)))
        q_bytes = _nbytes((bpt, HEAD_DIM, (IN_TN // HEAD_DIM) * MOBA_BLOCK), cdt)
    else:
        q_shape = jax.ShapeDtypeStruct((T, ATT_WIDTH), cdt)
        q_spec = pl.BlockSpec((tm, IN_TN), lambda i, j: (i, jq(j)))
        q_bytes = _nbytes((tm, IN_TN), cdt)
    kvt_spec = pl.BlockSpec((1, KV_WIDTH, tm), lambda i, j: (i // tiles_per_b, 0, i % tiles_per_b))
    out_shape = [
        q_shape,
        jax.ShapeDtypeStruct((n_batch, KV_WIDTH, S), F32),
        jax.ShapeDtypeStruct((n_batch, KV_WIDTH, S), F32),
        jax.ShapeDtypeStruct((T, LANES), F32),
    ]
    out_specs = [q_spec, kvt_spec, kvt_spec, pl.BlockSpec((tm, LANES), lambda i, j: (i, 0))]
    blk = (_nbytes((tm, D), F32) + _nbytes((D, IN_TN), cdt) + _nbytes((D, LANES), cdt)
           + 2 * _nbytes((tm, LANES), F32) + q_bytes + 2 * _nbytes((tm, KV_WIDTH), F32)
           + _nbytes((tm, LANES), F32))
    if q_blocks:
        kvb_shape = jax.ShapeDtypeStruct((n_batch, S // MOBA_BLOCK, KV_WIDTH, MOBA_BLOCK), cdt)
        kvb_spec = pl.BlockSpec((1, bpt, KV_WIDTH, MOBA_BLOCK),
                                lambda i, j: (i // tiles_per_b, i % tiles_per_b, 0, 0))
        out_shape += [kvb_shape, kvb_shape]
        out_specs += [kvb_spec, kvb_spec]
        blk += 2 * _nbytes((KV_WIDTH, tm), cdt)
    scr = _nbytes((tm, D), cdt)
    tmp = 4 * _nbytes((tm, IN_TN), F32)
    return pl.pallas_call(
        functools.partial(_inproj_qkv_kernel, precise=precise, q_scale=q_scale, q_blocks=q_blocks),
        grid=(T // tm, n_col),
        in_specs=in_specs,
        out_specs=out_specs,
        out_shape=out_shape,
        scratch_shapes=[pltpu.VMEM((tm, D), cdt)],
        compiler_params=pltpu.CompilerParams(
            dimension_semantics=("parallel", "arbitrary"),
            vmem_limit_bytes=_vmem_limit(blk, scr, tmp)),
        name="inproj_qkv",
    )(x2d, norm_w.reshape(1, D), w_qkv, w_dt, cos_t, sin_t)


def _moba_prompt_kernel(q_ref, ktf_ref, kt_ref, vt_ref, o_ref,
                        acc_scr, m_scr, l_scr, sel_scr, kmt_scr, *, nb_pad):
    i = pl.program_id(1)
    blk = MOBA_BLOCK
    gw = Q_PER_KV * blk

    def q_group(g):
        return q_ref[0, 0, :, g * gw:(g + 1) * gw]

    def scores(j, g):
        return _mm_tn(kt_ref[0, j, g * HEAD_DIM:(g + 1) * HEAD_DIM, :], q_group(g))

    def v_group(j, g):
        return vt_ref[0, j, g * HEAD_DIM:(g + 1) * HEAD_DIM, :]

    def own_block():
        @pl.when(i == 0)
        def _():
            kmt_scr[...] = jnp.zeros_like(kmt_scr)

        lane = lax.broadcasted_iota(jnp.int32, kmt_scr.shape, 1)
        kmt_scr[...] = jnp.where(lane == i, jnp.mean(ktf_ref[0], axis=1, keepdims=True), kmt_scr[...])
        n_iota = lax.broadcasted_iota(jnp.int32, (nb_pad, gw), 0)
        valid = n_iota < i
        key_iota = lax.broadcasted_iota(jnp.int32, (blk, blk), 0)
        qry_iota = lax.broadcasted_iota(jnp.int32, (blk, blk), 1)
        causal_bias = jnp.where(key_iota <= qry_iota, 0.0, NEG_INF)
        causal_bias = jnp.concatenate([causal_bias] * Q_PER_KV, axis=1)
        s_next = scores(i, 0)
        for g in range(N_KV_HEADS):
            s = s_next
            if g + 1 < N_KV_HEADS:
                s_next = scores(i, g + 1)
            gt = _mm_tn(kmt_scr[g * HEAD_DIM:(g + 1) * HEAD_DIM, :], q_group(g))[:nb_pad, :]
            cnt = jnp.zeros((nb_pad, gw), jnp.int32)
            for n2 in range(nb_pad):
                row = gt[n2:n2 + 1, :]
                beats = (row > gt) | ((row == gt) & (n2 < n_iota))
                cnt = cnt + jnp.where(beats & (n2 < i), 1, 0)
            sel_scr[:, g * gw:(g + 1) * gw] = jnp.where(valid & (cnt < MOBA_TOPK), 1.0, 0.0)
            s = s + causal_bias
            m = jnp.max(s, axis=0, keepdims=True)
            p = jnp.exp2(s - m)
            m_scr[g] = m
            l_scr[g] = jnp.sum(p, axis=0, keepdims=True)
            acc_scr[g] = _mm(v_group(i, g), p, False)

    def past_block(j, carry):
        s_next = scores(j, 0)
        for g in range(N_KV_HEADS):
            s = s_next
            if g + 1 < N_KV_HEADS:
                s_next = scores(j, g + 1)
            sel = sel_scr[pl.ds(j, 1), g * gw:(g + 1) * gw] > 0.5
            m_prev = m_scr[g]
            m_new = jnp.where(sel, jnp.maximum(m_prev, jnp.max(s, axis=0, keepdims=True)), m_prev)
            alpha = jnp.exp2(m_prev - m_new)
            p = jnp.exp2(s - jnp.where(sel, m_new, jnp.inf))
            l_scr[g] = alpha * l_scr[g] + jnp.sum(p, axis=0, keepdims=True)
            m_scr[g] = m_new
            acc_scr[g] = alpha * acc_scr[g] + _mm(v_group(j, g), p, False)
        return carry

    own_block()
    lax.fori_loop(0, i, past_block, 0)
    tiles = []
    for g in range(N_KV_HEADS):
        og = acc_scr[g] / l_scr[g]
        tiles += [og[:, r * blk:(r + 1) * blk] for r in range(Q_PER_KV)]
    o_ref[0] = jnp.concatenate(tiles, axis=0).T.astype(o_ref.dtype)


def _moba_prompt(qx, kt, ktb, vtb):
    B, nb, _, _ = qx.shape
    blk = MOBA_BLOCK
    S = nb * blk
    gw = Q_PER_KV * blk
    nb_pad = -(-nb // SUBLANES) * SUBLANES
    assert nb_pad <= LANES
    cdt = BF16
    grid_spec = pl.GridSpec(
        grid=(B, nb),
        in_specs=[
            pl.BlockSpec((1, 1, HEAD_DIM, N_HEADS * blk), lambda b, i: (b, i, 0, 0)),
            pl.BlockSpec((1, KV_WIDTH, blk), lambda b, i: (b, 0, i)),
            pl.BlockSpec((1, nb, KV_WIDTH, blk), lambda b, i: (b, 0, 0, 0)),
            pl.BlockSpec((1, nb, KV_WIDTH, blk), lambda b, i: (b, 0, 0, 0)),
        ],
        out_specs=pl.BlockSpec((1, blk, ATT_WIDTH), lambda b, i: (b, i, 0)),
        scratch_shapes=[
            pltpu.VMEM((N_KV_HEADS, HEAD_DIM, gw), F32),
            pltpu.VMEM((N_KV_HEADS, 1, gw), F32),
            pltpu.VMEM((N_KV_HEADS, 1, gw), F32),
            pltpu.VMEM((nb_pad, N_KV_HEADS * gw), F32),
            pltpu.VMEM((KV_WIDTH, LANES), F32),
        ],
    )
    blkb = (_nbytes((HEAD_DIM, N_HEADS * blk), cdt) + _nbytes((KV_WIDTH, blk), F32)
            + 2 * _nbytes((nb, KV_WIDTH, blk), cdt) + _nbytes((blk, ATT_WIDTH), cdt))
    scr = (_nbytes((N_KV_HEADS, HEAD_DIM, gw), F32) + 2 * _nbytes((N_KV_HEADS, SUBLANES, gw), F32)
           + _nbytes((nb_pad, N_KV_HEADS * gw), F32) + _nbytes((KV_WIDTH, LANES), F32))
    tmp = 6 * _nbytes((blk, gw), F32)
    return pl.pallas_call(
        functools.partial(_moba_prompt_kernel, nb_pad=nb_pad),
        grid_spec=grid_spec,
        out_shape=jax.ShapeDtypeStruct((B, S, ATT_WIDTH), cdt),
        compiler_params=pltpu.CompilerParams(
            dimension_semantics=("parallel", "arbitrary"),
            vmem_limit_bytes=_vmem_limit(blkb, scr, tmp)),
        name="moba_prompt",
    )(qx, kt, ktb, vtb)


PAGES_PER_BLOCK = MOBA_BLOCK // PAGE_SIZE
SAMPLE_BLOCKS_PER_STEP = 8


def _moba_sample_kernel(pt_ref, q_ref, *rest, n_blocks, n_steps, dec_seq):
    del pt_ref
    npg = SAMPLE_BLOCKS_PER_STEP * PAGES_PER_BLOCK
    k_pages = rest[:npg]
    v_pages = rest[npg:2 * npg]
    knew_ref, vnew_ref, o_ref, m_scr, l_scr, g_scr, o_scr = rest[2 * npg:]
    s_id = pl.program_id(1)
    qbd = q_ref[0]
    rows = qbd.shape[0]

    for kk in range(SAMPLE_BLOCKS_PER_STEP):
        n = s_id * SAMPLE_BLOCKS_PER_STEP + kk
        ktb = jnp.concatenate([k_pages[kk * PAGES_PER_BLOCK + p][...] for p in range(PAGES_PER_BLOCK)], axis=1)
        vtb = jnp.concatenate([v_pages[kk * PAGES_PER_BLOCK + p][...] for p in range(PAGES_PER_BLOCK)], axis=1)
        sc = _mm(qbd, ktb, False)
        g_scr[n] = jnp.mean(sc, axis=1, keepdims=True)
        m = jnp.max(sc, axis=1, keepdims=True)
        p = jnp.exp(sc - m)
        m_scr[n] = m
        l_scr[n] = jnp.sum(p, axis=1, keepdims=True)
        o_scr[n] = _mm_nt(p, vtb, False)

    @pl.when(s_id == n_steps - 1)
    def _():
        kn = knew_ref[0]
        vn = vnew_ref[0]
        qidx = lax.broadcasted_iota(jnp.int32, (rows, 1), 0) % dec_seq
        cols = []
        for tt in range(dec_seq):
            c = jnp.sum(qbd * kn[tt:tt + 1, :], axis=1, keepdims=True)
            cols.append(jnp.where(tt <= qidx, c, NEG_INF))
        m_own = cols[0]
        for c in cols[1:]:
            m_own = jnp.maximum(m_own, c)
        l_own = jnp.zeros((rows, 1), F32)
        o_own = jnp.zeros((rows, KV_WIDTH), F32)
        for tt in range(dec_seq):
            pt = jnp.exp(cols[tt] - m_own)
            l_own = l_own + pt
            o_own = o_own + pt * vn[tt:tt + 1, :]
        lane = lax.broadcasted_iota(jnp.int32, (rows, LANES), 1)
        gmat = jnp.zeros((rows, LANES), F32)
        mmat = jnp.zeros((rows, LANES), F32)
        lmat = jnp.zeros((rows, LANES), F32)
        for n in range(n_blocks):
            gmat = jnp.where(lane == n, g_scr[n], gmat)
            mmat = jnp.where(lane == n, m_scr[n], mmat)
            lmat = jnp.where(lane == n, l_scr[n], lmat)
        cnt = jnp.zeros((rows, LANES), jnp.int32)
        for n2 in range(n_blocks):
            col = g_scr[n2]
            beats = (col > gmat) | ((col == gmat) & (n2 < lane))
            cnt = cnt + jnp.where(beats, 1, 0)
        sel = (lane < n_blocks) & (cnt < MOBA_TOPK)
        m_all = jnp.maximum(m_own, jnp.max(jnp.where(sel, mmat, NEG_INF), axis=1, keepdims=True))
        wmat = jnp.where(sel, jnp.exp(mmat - m_all), 0.0)
        w_own = jnp.exp(m_own - m_all)
        denom = w_own * l_own + jnp.sum(wmat * lmat, axis=1, keepdims=True)
        out = w_own * o_own
        for n in range(n_blocks):
            out = out + wmat[:, n:n + 1] * o_scr[n]
        o_ref[0] = out / denom


def _moba_sample(qbd, cache_k, cache_v, page_table, k_new, v_new):
    Bd, rows, _ = qbd.shape
    dec_seq = k_new.shape[1]
    n_pages = page_table.shape[1]
    assert (n_pages * PAGE_SIZE) % MOBA_BLOCK == 0
    n_blocks = n_pages // PAGES_PER_BLOCK
    assert n_blocks % SAMPLE_BLOCKS_PER_STEP == 0 and n_blocks <= LANES
    n_steps = n_blocks // SAMPLE_BLOCKS_PER_STEP
    npg = SAMPLE_BLOCKS_PER_STEP * PAGES_PER_BLOCK

    def page_spec(p):
        return pl.BlockSpec((None, KV_WIDTH, PAGE_SIZE), lambda b, s, pt: (pt[b, s * npg + p], 0, 0))

    grid_spec = pltpu.PrefetchScalarGridSpec(
        num_scalar_prefetch=1,
        grid=(Bd, n_steps),
        in_specs=([pl.BlockSpec((1, rows, KV_WIDTH), lambda b, s, pt: (b, 0, 0))]
                  + [page_spec(p) for p in range(npg)] + [page_spec(p) for p in range(npg)]
                  + [pl.BlockSpec((1, dec_seq, KV_WIDTH), lambda b, s, pt: (b, 0, 0))] * 2),
        out_specs=pl.BlockSpec((1, rows, KV_WIDTH), lambda b, s, pt: (b, 0, 0)),
        scratch_shapes=[
            pltpu.VMEM((n_blocks, rows, 1), F32),
            pltpu.VMEM((n_blocks, rows, 1), F32),
            pltpu.VMEM((n_blocks, rows, 1), F32),
            pltpu.VMEM((n_blocks, rows, KV_WIDTH), F32),
        ],
    )
    blkb = (2 * _nbytes((rows, KV_WIDTH), F32) + 2 * npg * _nbytes((PAGE_SIZE, KV_WIDTH), F32)
            + 2 * _nbytes((SUBLANES, KV_WIDTH), F32))
    scr = 3 * _nbytes((n_blocks, rows, LANES), F32) + _nbytes((n_blocks, rows, KV_WIDTH), F32)
    tmp = 8 * _nbytes((MOBA_BLOCK, KV_WIDTH), F32)
    return pl.pallas_call(
        functools.partial(_moba_sample_kernel, n_blocks=n_blocks, n_steps=n_steps, dec_seq=dec_seq),
        grid_spec=grid_spec,
        out_shape=jax.ShapeDtypeStruct((Bd, rows, KV_WIDTH), F32),
        compiler_params=pltpu.CompilerParams(
            dimension_semantics=("parallel", "arbitrary"),
            vmem_limit_bytes=_vmem_limit(blkb, scr, tmp)),
        name="moba_sample",
    )(page_table, qbd, *([cache_k] * npg), *([cache_v] * npg), k_new, v_new)


CONV_PAD = SUBLANES


def _ssd_kernel(*refs, precise, n_valid, n_chunks, has_state, n_ssm_heads, d_inner):
    tc = SSD_CHUNK
    refs = list(refs)
    z_ref, xs_ref, bc_ref, dt_ref = refs[:4]
    del refs[:4]
    if has_state:
        h0_ref, pre_ref = refs[:2]
        del refs[:2]
    (cw_ref, cb_ref, dtb_ref, a_ref, dsk_ref, nw_ref, o_ref, st_ref, cv_ref,
     ext_scr, h_scr, xt_scr, xd_scr, yt_scr) = refs[:14]
    if n_valid < tc:
        dtp_scr, zp_scr = refs[14:]
    c = pl.program_id(1)
    gw = SSM_GROUPS * D_STATE
    heads_per_group = n_ssm_heads // SSM_GROUPS
    rows_per_group = heads_per_group * SSM_HEAD_DIM

    @pl.when(c == 0)
    def _():
        ext_scr[...] = jnp.zeros_like(ext_scr)
        if has_state:
            ext_scr[CONV_PAD - (CONV_WIDTH - 1):CONV_PAD, :] = pre_ref[0]
            h_scr[...] = h0_ref[0]
        else:
            h_scr[...] = jnp.zeros_like(h_scr)

    ext_scr[CONV_PAD:CONV_PAD + n_valid, :d_inner] = xs_ref[0]
    ext_scr[CONV_PAD:CONV_PAD + n_valid, d_inner:] = bc_ref[0]
    conv = cb_ref[...]
    for w in range(CONV_WIDTH):
        off = CONV_PAD - (CONV_WIDTH - 1) + w
        conv = conv + ext_scr[off:off + tc, :] * cw_ref[w:w + 1, :]
    xc = _silu(conv)

    @pl.when(c == n_chunks - 1)
    def _():
        lo = CONV_PAD + n_valid - (CONV_WIDTH - 1)
        cv_ref[0] = ext_scr[lo:lo + CONV_WIDTH - 1, :]

    ext_scr[CONV_PAD - (CONV_WIDTH - 1):CONV_PAD, :] = ext_scr[CONV_PAD + tc - (CONV_WIDTH - 1):CONV_PAD + tc, :]

    if n_valid == tc:
        dt_raw = dt_ref[0]
    else:
        dtp_scr[...] = jnp.zeros_like(dtp_scr)
        dtp_scr[0:n_valid, :] = dt_ref[0]
        dt_raw = dtp_scr[...]
    xdt_in = dt_raw + dtb_ref[...]
    dt = jnp.maximum(xdt_in, 0.0) + jnp.log1p(jnp.exp(-jnp.abs(xdt_in)))
    t_iota = lax.broadcasted_iota(jnp.int32, (tc, LANES), 0)
    dt = jnp.where(t_iota < n_valid, dt, 0.0)
    d_a = dt * a_ref[...]
    tri = (lax.broadcasted_iota(jnp.int32, (tc, tc), 0) >= lax.broadcasted_iota(jnp.int32, (tc, tc), 1))
    acum = jnp.dot(tri.astype(F32), d_a, precision=lax.Precision.HIGHEST, preferred_element_type=F32)
    dt_t = dt.T
    acum_t = acum.T
    a_last = acum_t[:, tc - 1:tc]
    exp_a_t = jnp.exp(acum_t)
    dec_t = jnp.exp(a_last - acum_t)
    chunk_dec = jnp.broadcast_to(jnp.exp(a_last), (LANES, D_STATE))

    xt_scr[...] = xc[:, :d_inner].T
    s_iota = lax.broadcasted_iota(jnp.int32, (tc, tc), 0)
    u_iota = lax.broadcasted_iota(jnp.int32, (tc, tc), 1)
    causal = s_iota <= u_iota
    for g in range(SSM_GROUPS):
        bg = xc[:, d_inner + g * D_STATE:d_inner + (g + 1) * D_STATE]
        cg = xc[:, d_inner + gw + g * D_STATE:d_inner + gw + (g + 1) * D_STATE]
        cbt = _mm_nt(bg, cg, precise)
        rs = slice(g * rows_per_group, (g + 1) * rows_per_group)
        hg = h_scr[rs, :]
        yoff = _mm_nt(hg, cg, precise)
        for hh in range(heads_per_group):
            h = g * heads_per_group + hh
            hs = slice(h * SSM_HEAD_DIM, (h + 1) * SSM_HEAD_DIM)
            xh = xt_scr[hs, :]
            xdt = xh * dt_t[h:h + 1, :]
            seg = acum_t[h:h + 1, :] - acum[:, h:h + 1]
            lmat = jnp.exp(jnp.where(causal, seg, NEG_INF))
            yd = _mm(xdt, cbt * lmat, precise)
            yt_scr[hs, :] = (yd + yoff[hh * SSM_HEAD_DIM:(hh + 1) * SSM_HEAD_DIM, :] * exp_a_t[h:h + 1, :]
                             + dsk_ref[h:h + 1, :] * xh)
            xd_scr[hs, :] = xdt * dec_t[h:h + 1, :]
        st = _mm(xd_scr[rs, :], bg, precise)
        for hh in range(heads_per_group):
            h = g * heads_per_group + hh
            hs = slice(h * SSM_HEAD_DIM, (h + 1) * SSM_HEAD_DIM)
            lo = hh * SSM_HEAD_DIM
            h_scr[hs, :] = hg[lo:lo + SSM_HEAD_DIM, :] * chunk_dec[h:h + 1, :] + st[lo:lo + SSM_HEAD_DIM, :]

    @pl.when(c == n_chunks - 1)
    def _():
        st_ref[0] = h_scr[...]

    y = yt_scr[...].T
    if n_valid == tc:
        zz = z_ref[0]
    else:
        zp_scr[...] = jnp.zeros_like(zp_scr)
        zp_scr[0:n_valid, :] = z_ref[0]
        zz = zp_scr[...]
    yz = y * _silu(zz)
    gsz = d_inner // SSM_GROUPS
    outs = []
    for g in range(SSM_GROUPS):
        grp = yz[:, g * gsz:(g + 1) * gsz]
        outs.append(grp * lax.rsqrt(jnp.mean(grp * grp, axis=-1, keepdims=True) + RMS_EPS))
    res = jnp.concatenate(outs, axis=1) * nw_ref[...]
    o_ref[0] = res[:n_valid, :].astype(o_ref.dtype)


def _ssd(main3, dt3, h0, prefix, conv_w, conv_b, dt_bias, a_log, d_skip, ssm_norm_w, *, precise):
    B, S, _ = main3.shape
    d_inner = ssm_norm_w.shape[0]
    conv_ch = conv_w.shape[1]
    bc_w = conv_ch - d_inner
    assert (2 * d_inner) % bc_w == 0
    n_heads = d_inner // SSM_HEAD_DIM
    tc = SSD_CHUNK
    if S % tc == 0:
        n_valid, n_chunks = tc, S // tc
    else:
        assert S < tc
        n_valid, n_chunks = S, 1
    has_state = h0 is not None
    cdt = F32 if precise else BF16
    pad_h = LANES - n_heads
    dtb = jnp.pad(dt_bias.astype(F32), (0, pad_h)).reshape(1, LANES)
    a_row = jnp.pad(-jnp.exp(a_log.astype(F32)), (0, pad_h)).reshape(1, LANES)
    dsk = jnp.broadcast_to(jnp.pad(d_skip.astype(F32), (0, pad_h))[:, None], (LANES, tc))

    full2 = lambda shp: pl.BlockSpec(shp, lambda b, c: (0,) * len(shp))
    in_specs = [
        pl.BlockSpec((1, n_valid, d_inner), lambda b, c: (b, c, 0)),
        pl.BlockSpec((1, n_valid, d_inner), lambda b, c: (b, c, 1)),
        pl.BlockSpec((1, n_valid, bc_w), lambda b, c: (b, c, 2 * d_inner // bc_w)),
        pl.BlockSpec((1, n_valid, LANES), lambda b, c: (b, c, 0)),
    ]
    args = [main3, main3, main3, dt3]
    if has_state:
        in_specs += [pl.BlockSpec((1, d_inner, D_STATE), lambda b, c: (b, 0, 0)),
                     pl.BlockSpec((1, CONV_WIDTH - 1, conv_ch), lambda b, c: (b, 0, 0))]
        args += [h0, prefix]
    in_specs += [full2((CONV_WIDTH, conv_ch)), full2((1, conv_ch)), full2((1, LANES)), full2((1, LANES)),
                 full2((LANES, tc)), full2((1, d_inner))]
    args += [conv_w, conv_b.reshape(1, conv_ch), dtb, a_row, dsk, ssm_norm_w.reshape(1, d_inner)]
    out_shape = [jax.ShapeDtypeStruct((B, S, d_inner), cdt),
                 jax.ShapeDtypeStruct((B, d_inner, D_STATE), F32),
                 jax.ShapeDtypeStruct((B, CONV_WIDTH - 1, conv_ch), F32)]
    out_specs = [pl.BlockSpec((1, n_valid, d_inner), lambda b, c: (b, c, 0)),
                 pl.BlockSpec((1, d_inner, D_STATE), lambda b, c: (b, 0, 0)),
                 pl.BlockSpec((1, CONV_WIDTH - 1, conv_ch), lambda b, c: (b, 0, 0))]
    scratch = [pltpu.VMEM((CONV_PAD + tc, conv_ch), F32), pltpu.VMEM((d_inner, D_STATE), F32),
               pltpu.VMEM((d_inner, tc), F32), pltpu.VMEM((d_inner, tc), F32), pltpu.VMEM((d_inner, tc), F32)]
    if n_valid < tc:
        scratch += [pltpu.VMEM((tc, LANES), F32), pltpu.VMEM((tc, d_inner), F32)]
    blk = (_nbytes((tc, d_inner), F32) + _nbytes((tc, conv_ch), F32) + _nbytes((tc, LANES), F32)
           + 2 * _nbytes((d_inner, D_STATE), F32) + 2 * _nbytes((SUBLANES, conv_ch), F32)
           + _nbytes((CONV_WIDTH + SUBLANES, conv_ch), F32) + _nbytes((tc, d_inner), cdt))
    scr = _nbytes((CONV_PAD + tc, conv_ch), F32) + 4 * _nbytes((d_inner, D_STATE), F32)
    tmp = 3 * _nbytes((tc, conv_ch), F32) + 4 * _nbytes((tc, d_inner), F32)
    return pl.pallas_call(
        functools.partial(_ssd_kernel, precise=precise, n_valid=n_valid, n_chunks=n_chunks,
                          has_state=has_state, n_ssm_heads=n_heads, d_inner=d_inner),
        grid=(B, n_chunks),
        in_specs=in_specs,
        out_specs=out_specs,
        out_shape=out_shape,
        scratch_shapes=scratch,
        compiler_params=pltpu.CompilerParams(
            dimension_semantics=("parallel", "arbitrary"),
            vmem_limit_bytes=_vmem_limit(blk, scr, tmp)),
        name="ssd",
    )(*args)


def _merge_kernel(x_ref, att_ref, ssd_ref, ga_ref, gb_ref, wpa_ref, wpb_ref, wo_ref, o_ref, *, precise):
    a = _mm(att_ref[...], wpa_ref[...], precise)
    b = _mm(ssd_ref[...], wpb_ref[...], precise)
    m = _sigmoid(ga_ref[...]) * a + _sigmoid(gb_ref[...]) * b
    o_ref[...] = x_ref[...] + _mm(m, wo_ref[...], precise)


def _merge(x2d, att, ssd, main2, gate_col, w_pa, w_pb, w_o, *, precise):
    T, D = x2d.shape
    d_inner = ssd.shape[1]
    tm = min(512, T)
    assert T % tm == 0 and gate_col % D == 0
    cdt = F32 if precise else BF16
    row = lambda w: pl.BlockSpec((tm, w), lambda i: (i, 0))
    gate = lambda k: pl.BlockSpec((tm, D), lambda i: (i, gate_col // D + k))
    full = lambda r, w: pl.BlockSpec((r, w), lambda i: (0, 0))
    blk = (4 * _nbytes((tm, D), F32) + _nbytes((tm, ATT_WIDTH), cdt) + _nbytes((tm, d_inner), cdt)
           + _nbytes((ATT_WIDTH + d_inner + D, D), cdt))
    tmp = 4 * _nbytes((tm, D), F32)
    return pl.pallas_call(
        functools.partial(_merge_kernel, precise=precise),
        grid=(T // tm,),
        in_specs=[row(D), row(ATT_WIDTH), row(d_inner), gate(0), gate(1),
                  full(ATT_WIDTH, D), full(d_inner, D), full(D, D)],
        out_specs=row(D),
        out_shape=jax.ShapeDtypeStruct((T, D), F32),
        compiler_params=pltpu.CompilerParams(
            dimension_semantics=("parallel",),
            vmem_limit_bytes=_vmem_limit(blk, 0, tmp)),
        name="merge",
    )(x2d, att, ssd, main2, main2, w_pa, w_pb, w_o)


def _moe_kernel(x_ref, n2_ref, wr_ref, wg_ref, wu_ref, wd_ref, nf_ref, o_ref, h_scr, gate_scr, acc_scr,
                *, precise, final_norm):
    e = pl.program_id(1)
    tm = x_ref.shape[0]
    epg = EXPERTS_PER_GROUP

    @pl.when(e == 0)
    def _():
        x = x_ref[...]
        h = x * lax.rsqrt(jnp.mean(x * x, axis=-1, keepdims=True) + RMS_EPS) * n2_ref[...]
        h_scr[...] = h.astype(h_scr.dtype)
        acc_scr[...] = jnp.zeros_like(acc_scr)
        lg = _mm_nt(wr_ref[...], h, precise)
        gl = [lg[k:k + 1, :] for k in range(N_EXPERT_GROUPS)]
        gmax = functools.reduce(jnp.maximum, gl)
        taken = jnp.zeros((1, tm), jnp.bool_)
        is_g = []
        for k in range(N_EXPERT_GROUPS):
            hit = (gl[k] == gmax) & (~taken)
            is_g.append(hit)
            taken = taken | hit
        p_grp = 1.0 / functools.reduce(lambda a, b: a + b, [jnp.exp(v - gmax) for v in gl])
        le = []
        for k in range(epg):
            v = jnp.zeros((1, tm), F32)
            for gi in range(N_EXPERT_GROUPS):
                r0 = N_EXPERT_GROUPS + gi * epg + k
                v = jnp.where(is_g[gi], lg[r0:r0 + 1, :], v)
            le.append(v)

        def first_max(vals):
            mx = functools.reduce(jnp.maximum, vals)
            tk = jnp.zeros((1, tm), jnp.bool_)
            hits = []
            for v in vals:
                hit = (v == mx) & (~tk)
                hits.append(hit)
                tk = tk | hit
            return mx, hits

        v1, o1 = first_max(le)
        v2, o2 = first_max([jnp.where(o1[k], NEG_INF, le[k]) for k in range(epg)])
        e21 = jnp.exp(v2 - v1)
        w1 = p_grp / (1.0 + e21)
        w2 = p_grp * e21 / (1.0 + e21)
        wk = [jnp.where(o1[k], w1, jnp.where(o2[k], w2, 0.0)) for k in range(epg)]
        row = lax.broadcasted_iota(jnp.int32, (LANES, tm), 0)
        for gi in range(N_EXPERT_GROUPS):
            gm = jnp.zeros((LANES, tm), F32)
            for k in range(epg):
                gm = jnp.where(row == k, jnp.where(is_g[gi], wk[k], 0.0), gm)
            gate_scr[gi] = gm.T

    h = h_scr[...]
    gate = gate_scr[e]
    cdt = F32 if precise else BF16
    acc = acc_scr[...]
    for k in range(epg):
        hg = _mm(h, wg_ref[k], precise)
        hu = _mm(h, wu_ref[k], precise)
        hid = (_silu(hg) * hu * gate[:, k:k + 1]).astype(cdt)
        acc = acc + _mm(hid, wd_ref[k], precise)
    acc_scr[...] = acc

    @pl.when(e == N_EXPERT_GROUPS - 1)
    def _():
        y = x_ref[...] + acc_scr[...]
        if final_norm:
            y = y * lax.rsqrt(jnp.mean(y * y, axis=-1, keepdims=True) + RMS_EPS) * nf_ref[...]
        o_ref[...] = y


def _moe(x2d, norm2_w, w_router, w_g, w_u, w_d, norm_f_w, *, precise, final_norm):
    T, D = x2d.shape
    n_e, _, d_e = w_g.shape
    assert n_e == N_EXPERTS
    tm = min(1024, T)
    assert T % tm == 0
    cdt = F32 if precise else BF16
    epg = EXPERTS_PER_GROUP
    blk = (2 * _nbytes((tm, D), F32) + 2 * _nbytes((1, D), F32) + _nbytes((D, LANES), F32)
           + 3 * _nbytes((epg, D, d_e), cdt))
    scr = _nbytes((tm, D), cdt) + _nbytes((N_EXPERT_GROUPS, tm, LANES), F32) + _nbytes((tm, D), F32)
    tmp = 6 * _nbytes((tm, d_e), F32) + 2 * _nbytes((tm, D), F32) + 64 * _nbytes((tm, LANES), F32)
    return pl.pallas_call(
        functools.partial(_moe_kernel, precise=precise, final_norm=final_norm),
        grid=(T // tm, N_EXPERT_GROUPS),
        in_specs=[
            pl.BlockSpec((tm, D), lambda i, e: (i, 0)),
            pl.BlockSpec((1, D), lambda i, e: (0, 0)),
            pl.BlockSpec((LANES, D), lambda i, e: (0, 0)),
            pl.BlockSpec((epg, D, d_e), lambda i, e: (e, 0, 0)),
            pl.BlockSpec((epg, D, d_e), lambda i, e: (e, 0, 0)),
            pl.BlockSpec((epg, d_e, D), lambda i, e: (e, 0, 0)),
            pl.BlockSpec((1, D), lambda i, e: (0, 0)),
        ],
        out_specs=pl.BlockSpec((tm, D), lambda i, e: (i, 0)),
        out_shape=jax.ShapeDtypeStruct((T, D), F32),
        scratch_shapes=[pltpu.VMEM((tm, D), cdt), pltpu.VMEM((N_EXPERT_GROUPS, tm, LANES), F32),
                        pltpu.VMEM((tm, D), F32)],
        compiler_params=pltpu.CompilerParams(
            dimension_semantics=("parallel", "arbitrary"),
            vmem_limit_bytes=_vmem_limit(blk, scr, tmp)),
        name="moe",
    )(x2d, norm2_w.reshape(1, D), w_router, w_g, w_u, w_d, norm_f_w.reshape(1, D))


def _rope_tables(pos):
    inv_freq = 1.0 / (ROPE_THETA ** (jnp.arange(0, HEAD_DIM, 2, dtype=F32) / HEAD_DIM))
    ang = pos.astype(F32)[:, None] * inv_freq[None, :]
    ang = jnp.concatenate([ang, ang], axis=-1)
    sign = jnp.concatenate([-jnp.ones((HEAD_DIM // 2,), F32), jnp.ones((HEAD_DIM // 2,), F32)])
    reps = LANES // HEAD_DIM
    return jnp.tile(jnp.cos(ang), (1, reps)), jnp.tile(jnp.sin(ang) * sign, (1, reps))


def _layer_weights(lw, precise):
    (norm1_w, w_in, conv_w, conv_b, dt_bias, a_log, d_skip, ssm_norm_w,
     w_pa, w_pb, w_o, norm2_w, w_rg, w_re, w_g, w_u, w_d) = lw
    cdt = F32 if precise else BF16
    D = w_in.shape[0]
    d_inner = ssm_norm_w.shape[0]
    conv_ch = conv_w.shape[1]
    n_heads = dt_bias.shape[0]
    qkv_w = ATT_WIDTH + 2 * KV_WIDTH
    dt0 = qkv_w + d_inner + conv_ch
    w_qkv = w_in[:, :qkv_w].astype(cdt)
    w_rest = jnp.concatenate([w_in[:, qkv_w:dt0], w_in[:, dt0 + n_heads:]], axis=1).astype(cdt)
    w_dt = jnp.pad(w_in[:, dt0:dt0 + n_heads], ((0, 0), (0, LANES - n_heads))).astype(cdt)
    w_router = jnp.pad(jnp.concatenate([w_rg, w_re], axis=1).T,
                       ((0, LANES - N_EXPERT_GROUPS - N_EXPERTS), (0, 0))).astype(F32)
    return dict(norm1_w=norm1_w, w_qkv=w_qkv, w_rest=w_rest, w_dt=w_dt, conv_w=conv_w, conv_b=conv_b,
                dt_bias=dt_bias,
                a_log=a_log, d_skip=d_skip, ssm_norm_w=ssm_norm_w, w_pa=w_pa.astype(cdt),
                w_pb=w_pb.astype(cdt), w_o=w_o.astype(cdt), norm2_w=norm2_w, w_router=w_router,
                w_g=w_g.astype(cdt), w_u=w_u.astype(cdt), w_d=w_d.astype(cdt),
                d_inner=d_inner, conv_ch=conv_ch)


def _pages_transposed(cache):
    n_phys = cache.shape[0]
    return cache.transpose(0, 2, 3, 1).reshape(n_phys, KV_WIDTH, PAGE_SIZE)


def _prompt_layer(x3, W, norm_f_w, final_norm):
    B, S, D = x3.shape
    T = B * S
    cos_t, sin_t = _rope_tables(jnp.arange(S, dtype=jnp.int32))
    x2d = x3.reshape(T, D)
    qx, kt, vt, dt, ktb, vtb = _inproj_qkv(
        x2d, B, W["norm1_w"], W["w_qkv"], W["w_dt"], cos_t, sin_t,
        precise=False, q_scale=ATT_SCALE * LOG2E, q_blocks=True)
    main = _inproj_main(x2d, W["norm1_w"], W["w_rest"], precise=False)
    att = _moba_prompt(qx, kt, ktb, vtb)
    k = kt.reshape(B, N_KV_HEADS, HEAD_DIM, S).transpose(0, 3, 1, 2)
    v = vt.reshape(B, N_KV_HEADS, HEAD_DIM, S).transpose(0, 3, 1, 2)
    ssd, state, conv_new = _ssd(
        main.reshape(B, S, -1), dt.reshape(B, S, LANES), None, None,
        W["conv_w"], W["conv_b"], W["dt_bias"], W["a_log"], W["d_skip"], W["ssm_norm_w"], precise=False)
    x1 = _merge(x2d, att.reshape(T, ATT_WIDTH), ssd.reshape(T, -1), main, W["d_inner"] + W["conv_ch"],
                W["w_pa"], W["w_pb"], W["w_o"], precise=False)
    y = _moe(x1, W["norm2_w"], W["w_router"], W["w_g"], W["w_u"], W["w_d"], norm_f_w,
             precise=False, final_norm=final_norm)
    return y.reshape(B, S, D), k, v, state, conv_new


def _sample_layer(x3, W, norm_f_w, final_norm, cache_k, cache_v, page_table, h0, prefix):
    Bd, Sd, D = x3.shape
    T = Bd * Sd
    past_len = page_table.shape[1] * PAGE_SIZE
    pos = past_len + (jnp.arange(T, dtype=jnp.int32) % Sd)
    cos_t, sin_t = _rope_tables(pos)
    x2d = x3.reshape(T, D)
    q, kt, vt, dt = _inproj_qkv(
        x2d, 1, W["norm1_w"], W["w_qkv"], W["w_dt"], cos_t, sin_t,
        precise=True, q_scale=ATT_SCALE, q_blocks=False)
    main = _inproj_main(x2d, W["norm1_w"], W["w_rest"], precise=True)
    k = kt[0].T
    v = vt[0].T
    q = q.reshape(Bd, Sd, N_HEADS, HEAD_DIM).transpose(0, 2, 1, 3)
    grp_of_head = jnp.arange(N_HEADS) // Q_PER_KV
    onehot = (grp_of_head[:, None] == jnp.arange(N_KV_HEADS)[None, :]).astype(F32)
    qbd = (q[:, :, :, None, :] * onehot[None, :, None, :, None]).reshape(Bd, N_HEADS * Sd, KV_WIDTH)
    o = _moba_sample(qbd, cache_k, cache_v, page_table,
                     k.reshape(Bd, Sd, KV_WIDTH), v.reshape(Bd, Sd, KV_WIDTH))
    o = o.reshape(Bd, N_HEADS, Sd, N_KV_HEADS, HEAD_DIM)
    o = jnp.sum(o * onehot[None, :, None, :, None], axis=3)
    att = o.transpose(0, 2, 1, 3).reshape(T, ATT_WIDTH)
    ssd, state, conv_new = _ssd(
        main.reshape(Bd, Sd, -1), dt.reshape(Bd, Sd, LANES), h0, prefix,
        W["conv_w"], W["conv_b"], W["dt_bias"], W["a_log"], W["d_skip"], W["ssm_norm_w"], precise=True)
    x1 = _merge(x2d, att, ssd.reshape(T, -1), main, W["d_inner"] + W["conv_ch"],
                W["w_pa"], W["w_pb"], W["w_o"], precise=True)
    y = _moe(x1, W["norm2_w"], W["w_router"], W["w_g"], W["w_u"], W["w_d"], norm_f_w,
             precise=True, final_norm=final_norm)
    return y.reshape(Bd, Sd, D), k, v, state, conv_new


def kernel(x_prompt, x_sample, cache_k, cache_v, state_ssm, state_conv, page_table, norm1_w, w_in, conv_w,
           conv_b, dt_bias, a_log, d_skip, ssm_norm_w, w_pa, w_pb, w_o, norm2_w, w_router_group,
           w_router_expert, w_gate_e, w_up_e, w_down_e, norm_f_w):
    depth = w_in.shape[0]
    B, S, D = x_prompt.shape
    Bd, Sd, _ = x_sample.shape
    n_phys = cache_k.shape[1]
    hp, hs = x_prompt, x_sample
    outs = [[] for _ in range(8)]
    for l in range(depth):
        lw = (norm1_w[l], w_in[l], conv_w[l], conv_b[l], dt_bias[l], a_log[l], d_skip[l], ssm_norm_w[l],
              w_pa[l], w_pb[l], w_o[l], norm2_w[l], w_router_group[l], w_router_expert[l],
              w_gate_e[l], w_up_e[l], w_down_e[l])
        last = l == depth - 1
        hp, kp, vp, sp, cp = _prompt_layer(hp, _layer_weights(lw, False), norm_f_w, last)
        n_ssm_heads, ssm_hd, d_state = state_ssm.shape[2:]
        hs, ks, vs, ss, cs = _sample_layer(
            hs, _layer_weights(lw, True), norm_f_w, last,
            _pages_transposed(cache_k[l]), _pages_transposed(cache_v[l]),
            page_table, state_ssm[l].reshape(Bd, n_ssm_heads * ssm_hd, d_state), state_conv[l])
        outs[0].append(kp)
        outs[1].append(vp)
        outs[2].append(sp.reshape(B, n_ssm_heads, ssm_hd, d_state))
        outs[3].append(cp)
        outs[4].append(ks.reshape(Bd, Sd, N_KV_HEADS, HEAD_DIM))
        outs[5].append(vs.reshape(Bd, Sd, N_KV_HEADS, HEAD_DIM))
        outs[6].append(ss.reshape(Bd, n_ssm_heads, ssm_hd, d_state))
        outs[7].append(cs)
    return (hp, hs) + tuple(jnp.stack(o) for o in outs)
```

```python
import functools
import math

import numpy as np
import jax
import jax.numpy as jnp
from jax import lax
from jax.experimental import pallas as pl
from jax.experimental.pallas import tpu as pltpu

N_HEADS = 16
HEAD_DIM = 64
N_KV_HEADS = 4
Q_PER_KV = N_HEADS // N_KV_HEADS
ATT_WIDTH = N_HEADS * HEAD_DIM
KV_WIDTH = N_KV_HEADS * HEAD_DIM
MOBA_BLOCK = 256
MOBA_TOPK = 3
ROPE_THETA = 10000.0
ATT_SCALE = 1.0 / math.sqrt(HEAD_DIM)
LOG2E = math.log2(math.e)
SSM_HEAD_DIM = 64
SSM_GROUPS = 4
D_STATE = 128
CONV_WIDTH = 4
SSD_CHUNK = 128
N_EXPERT_GROUPS = 4
EXPERTS_PER_GROUP = 4
N_EXPERTS = N_EXPERT_GROUPS * EXPERTS_PER_GROUP
RMS_EPS = 1e-6
PAGE_SIZE = 128

LANES = 128
SUBLANES = 8
V7X_VMEM_BYTES = 64 * 1024 * 1024
VMEM_COMPILER_RESERVE = 8 * 1024 * 1024

F32 = jnp.float32
BF16 = jnp.bfloat16
NEG_INF = float("-inf")


def _vmem_limit(block_bytes, scratch_bytes, temp_bytes):
    need = 2 * block_bytes + scratch_bytes + temp_bytes + VMEM_COMPILER_RESERVE
    return int(min(need, V7X_VMEM_BYTES - VMEM_COMPILER_RESERVE))


def _nbytes(shape, dtype):
    return int(np.prod(shape)) * jnp.dtype(dtype).itemsize


def _mm(a, b, precise):
    if precise:
        return jnp.dot(a.astype(F32), b.astype(F32), precision=lax.Precision.HIGHEST,
                       preferred_element_type=F32)
    return jnp.dot(a.astype(BF16), b.astype(BF16), preferred_element_type=F32)


def _mm_nt(a, b, precise):
    dn = (((1,), (1,)), ((), ()))
    if precise:
        return lax.dot_general(a.astype(F32), b.astype(F32), dn, precision=lax.Precision.HIGHEST,
                               preferred_element_type=F32)
    return lax.dot_general(a.astype(BF16), b.astype(BF16), dn, preferred_element_type=F32)


def _mm_tn(a, b, precise=False):
    dn = (((0,), (0,)), ((), ()))
    if precise:
        return lax.dot_general(a.astype(F32), b.astype(F32), dn, precision=lax.Precision.HIGHEST,
                               preferred_element_type=F32)
    return lax.dot_general(a.astype(BF16), b.astype(BF16), dn, preferred_element_type=F32)


def _sigmoid(x):
    return 1.0 / (1.0 + jnp.exp(-x))


def _silu(x):
    return x * _sigmoid(x)


IN_TN = 512
IN_Q_TILES = ATT_WIDTH // IN_TN
IN_KV_TILE = IN_Q_TILES


def _norm_to_scratch(x_ref, nw_ref, h_scr):
    x = x_ref[...]
    h = x * lax.rsqrt(jnp.mean(x * x, axis=-1, keepdims=True) + RMS_EPS)
    h_scr[...] = (h * nw_ref[...]).astype(h_scr.dtype)


def _inproj_main_kernel(x_ref, nw_ref, w_ref, o_ref, h_scr, *, precise):
    @pl.when(pl.program_id(1) == 0)
    def _():
        _norm_to_scratch(x_ref, nw_ref, h_scr)

    o_ref[...] = _mm(h_scr[...], w_ref[...], precise)


def _inproj_qkv_kernel(x_ref, nw_ref, w_ref, wdt_ref, cos_ref, sin_ref, q_ref, kt_ref, vt_ref, dt_ref, *rest,
                       precise, q_scale, q_blocks):
    if q_blocks:
        ktb_ref, vtb_ref, h_scr = rest
    else:
        (h_scr,) = rest
    j = pl.program_id(1)
    tm = x_ref.shape[0]

    @pl.when(j == 0)
    def _():
        _norm_to_scratch(x_ref, nw_ref, h_scr)

    def rope(a):
        cos = cos_ref[...]
        sin = sin_ref[...]
        lane = lax.broadcasted_iota(jnp.int32, (tm, LANES), 1)
        first_half = (lane % HEAD_DIM) < (HEAD_DIM // 2)
        outs = []
        for c in range(a.shape[1] // LANES):
            xc = a[:, c * LANES:(c + 1) * LANES]
            rot = jnp.where(first_half, pltpu.roll(xc, LANES - HEAD_DIM // 2, 1),
                            pltpu.roll(xc, HEAD_DIM // 2, 1))
            outs.append(xc * cos + rot * sin)
        return jnp.concatenate(outs, axis=1)

    @pl.when(j < IN_KV_TILE)
    def _():
        qr = rope(_mm(h_scr[...], w_ref[...], precise)) * q_scale
        if q_blocks:
            qrt = qr.T.astype(q_ref.dtype)
            for hh in range(IN_TN // HEAD_DIM):
                for bi in range(tm // MOBA_BLOCK):
                    q_ref[0, bi, :, hh * MOBA_BLOCK:(hh + 1) * MOBA_BLOCK] = (
                        qrt[hh * HEAD_DIM:(hh + 1) * HEAD_DIM, bi * MOBA_BLOCK:(bi + 1) * MOBA_BLOCK])
        else:
            q_ref[...] = qr.astype(q_ref.dtype)

    @pl.when(j == IN_KV_TILE)
    def _():
        acc = _mm(h_scr[...], w_ref[...], precise)
        kt = rope(acc[:, :KV_WIDTH]).T
        vt = acc[:, KV_WIDTH:].T
        kt_ref[0] = kt
        vt_ref[0] = vt
        if q_blocks:
            ones = jnp.ones((V_ROWS - HEAD_DIM, MOBA_BLOCK), vtb_ref.dtype)
            for bi in range(tm // MOBA_BLOCK):
                cols = slice(bi * MOBA_BLOCK, (bi + 1) * MOBA_BLOCK)
                ktb_ref[0, bi] = kt[:, cols].astype(ktb_ref.dtype)
                for g in range(N_KV_HEADS):
                    vtb_ref[0, bi, g * V_ROWS:g * V_ROWS + HEAD_DIM, :] = (
                        vt[g * HEAD_DIM:(g + 1) * HEAD_DIM, cols].astype(vtb_ref.dtype))
                    vtb_ref[0, bi, g * V_ROWS + HEAD_DIM:(g + 1) * V_ROWS, :] = ones
        dt_ref[...] = _mm(h_scr[...], wdt_ref[...], precise)


IN_MAIN_TN = 1024


def _inproj_main(x2d, norm_w, w_rest, *, precise):
    T, D = x2d.shape
    n_rest = w_rest.shape[1]
    tm = min(1024, T)
    assert T % tm == 0 and n_rest % IN_MAIN_TN == 0
    cdt = F32 if precise else BF16
    blk = _nbytes((tm, D), F32) + _nbytes((D, IN_MAIN_TN), cdt) + _nbytes((tm, IN_MAIN_TN), F32)
    return pl.pallas_call(
        functools.partial(_inproj_main_kernel, precise=precise),
        grid=(T // tm, n_rest // IN_MAIN_TN),
        in_specs=[pl.BlockSpec((tm, D), lambda i, j: (i, 0)),
                  pl.BlockSpec((1, D), lambda i, j: (0, 0)),
                  pl.BlockSpec((D, IN_MAIN_TN), lambda i, j: (0, j))],
        out_specs=pl.BlockSpec((tm, IN_MAIN_TN), lambda i, j: (i, j)),
        out_shape=jax.ShapeDtypeStruct((T, n_rest), F32),
        scratch_shapes=[pltpu.VMEM((tm, D), cdt)],
        compiler_params=pltpu.CompilerParams(
            dimension_semantics=("parallel", "arbitrary"),
            vmem_limit_bytes=_vmem_limit(blk, _nbytes((tm, D), cdt), _nbytes((tm, IN_MAIN_TN), F32))),
        name="inproj_main",
    )(x2d, norm_w.reshape(1, D), w_rest)


def _inproj_qkv(x2d, n_batch, norm_w, w_qkv, w_dt, cos_t, sin_t, *, precise, q_scale, q_blocks):
    T, D = x2d.shape
    S = T // n_batch
    tm = min(1024, S)
    assert S % tm == 0 and ATT_WIDTH % IN_TN == 0 and 2 * KV_WIDTH == IN_TN
    assert not q_blocks or tm % MOBA_BLOCK == 0
    tiles_per_b = S // tm
    n_col = IN_KV_TILE + 1
    assert w_qkv.shape == (D, n_col * IN_TN)
    cdt = F32 if precise else BF16

    in_specs = [
        pl.BlockSpec((tm, D), lambda i, j: (i, 0)),
        pl.BlockSpec((1, D), lambda i, j: (0, 0)),
        pl.BlockSpec((D, IN_TN), lambda i, j: (0, j)),
        pl.BlockSpec((D, LANES), lambda i, j: (0, 0)),
        pl.BlockSpec((tm, LANES), lambda i, j: (i % tiles_per_b, 0)),
        pl.BlockSpec((tm, LANES), lambda i, j: (i % tiles_per_b, 0)),
    ]
    jq = lambda j: jnp.clip(j, 0, IN_Q_TILES - 1)
    if q_blocks:
        bpt = tm // MOBA_BLOCK
        q_shape = jax.ShapeDtypeStruct((n_batch, S // MOBA_BLOCK, HEAD_DIM, N_HEADS * MOBA_BLOCK), cdt)
        q_spec = pl.BlockSpec((1, bpt, HEAD_DIM, (IN_TN // HEAD_DIM) * MOBA_BLOCK),
                              lambda i, j: (i // tiles_per_b, i % tiles_per_b, 0, jq(j)))
        q_bytes = _nbytes((bpt, HEAD_DIM, (IN_TN // HEAD_DIM) * MOBA_BLOCK), cdt)
    else:
        q_shape = jax.ShapeDtypeStruct((T, ATT_WIDTH), cdt)
        q_spec = pl.BlockSpec((tm, IN_TN), lambda i, j: (i, jq(j)))
        q_bytes = _nbytes((tm, IN_TN), cdt)
    kvt_spec = pl.BlockSpec((1, KV_WIDTH, tm), lambda i, j: (i // tiles_per_b, 0, i % tiles_per_b))
    out_shape = [
        q_shape,
        jax.ShapeDtypeStruct((n_batch, KV_WIDTH, S), F32),
        jax.ShapeDtypeStruct((n_batch, KV_WIDTH, S), F32),
        jax.ShapeDtypeStruct((T, LANES), F32),
    ]
    out_specs = [q_spec, kvt_spec, kvt_spec, pl.BlockSpec((tm, LANES), lambda i, j: (i, 0))]
    blk = (_nbytes((tm, D), F32) + _nbytes((D, IN_TN), cdt) + _nbytes((D, LANES), cdt)
           + 2 * _nbytes((tm, LANES), F32) + q_bytes + 2 * _nbytes((tm, KV_WIDTH), F32)
           + _nbytes((tm, LANES), F32))
    if q_blocks:
        for rows in (KV_WIDTH, N_KV_HEADS * V_ROWS):
            out_shape.append(jax.ShapeDtypeStruct((n_batch, S // MOBA_BLOCK, rows, MOBA_BLOCK), cdt))
            out_specs.append(pl.BlockSpec((1, bpt, rows, MOBA_BLOCK),
                                          lambda i, j: (i // tiles_per_b, i % tiles_per_b, 0, 0)))
            blk += _nbytes((rows, tm), cdt)
    scr = _nbytes((tm, D), cdt)
    tmp = 4 * _nbytes((tm, IN_TN), F32)
    return pl.pallas_call(
        functools.partial(_inproj_qkv_kernel, precise=precise, q_scale=q_scale, q_blocks=q_blocks),
        grid=(T // tm, n_col),
        in_specs=in_specs,
        out_specs=out_specs,
        out_shape=out_shape,
        scratch_shapes=[pltpu.VMEM((tm, D), cdt)],
        compiler_params=pltpu.CompilerParams(
            dimension_semantics=("parallel", "arbitrary"),
            vmem_limit_bytes=_vmem_limit(blk, scr, tmp)),
        name="inproj_qkv",
    )(x2d, norm_w.reshape(1, D), w_qkv, w_dt, cos_t, sin_t)


BF16_SUBLANES = 2 * SUBLANES
V_ROWS = HEAD_DIM + BF16_SUBLANES


def _moba_prompt_kernel(q_ref, ktf_ref, kt_ref, vt_ref, o_ref,
                        acc_scr, m_scr, sel_scr, kmt_scr, *, nb_pad):
    i = pl.program_id(1)
    blk = MOBA_BLOCK
    gw = Q_PER_KV * blk

    def q_group(g):
        return q_ref[0, 0, :, g * gw:(g + 1) * gw]

    def scores(j, g):
        return _mm_tn(kt_ref[0, j, g * HEAD_DIM:(g + 1) * HEAD_DIM, :], q_group(g))

    def v_group(j, g):
        return vt_ref[0, j, g * V_ROWS:(g + 1) * V_ROWS, :]

    def own_block():
        @pl.when(i == 0)
        def _():
            kmt_scr[...] = jnp.zeros_like(kmt_scr)

        lane = lax.broadcasted_iota(jnp.int32, kmt_scr.shape, 1)
        kmt_scr[...] = jnp.where(lane == i, jnp.mean(ktf_ref[0], axis=1, keepdims=True), kmt_scr[...])
        n_iota = lax.broadcasted_iota(jnp.int32, (nb_pad, gw), 0)
        valid = n_iota < i
        key_iota = lax.broadcasted_iota(jnp.int32, (blk, blk), 0)
        qry_iota = lax.broadcasted_iota(jnp.int32, (blk, blk), 1)
        causal_bias = jnp.where(key_iota <= qry_iota, 0.0, NEG_INF)
        causal_bias = jnp.concatenate([causal_bias] * Q_PER_KV, axis=1)
        s_next = scores(i, 0)
        for g in range(N_KV_HEADS):
            s = s_next
            if g + 1 < N_KV_HEADS:
                s_next = scores(i, g + 1)
            gt = _mm_tn(kmt_scr[g * HEAD_DIM:(g + 1) * HEAD_DIM, :], q_group(g))[:nb_pad, :]
            gt = jnp.where(valid, gt, NEG_INF)
            picked = jnp.zeros((nb_pad, gw), jnp.bool_)
            for _ in range(MOBA_TOPK):
                top = jnp.max(gt, axis=0, keepdims=True)
                first = jnp.min(jnp.where(gt == top, n_iota, nb_pad), axis=0, keepdims=True)
                pick = n_iota == first
                picked = picked | pick
                gt = jnp.where(pick, NEG_INF, gt)
            sel_scr[:, g * gw:(g + 1) * gw] = jnp.where(valid & picked, 1.0, 0.0)
            s = s + causal_bias
            m = jnp.max(s, axis=0, keepdims=True)
            p = jnp.exp2(s - m)
            m_scr[g] = m
            acc_scr[g] = _mm(v_group(i, g), p, False)

    def past_blocks(js):
        units = [(j, g) for j in js for g in range(N_KV_HEADS)]
        s_next = scores(*units[0])
        for u, (j, g) in enumerate(units):
            s = s_next
            if u + 1 < len(units):
                s_next = scores(*units[u + 1])
            sel = sel_scr[pl.ds(j, 1), g * gw:(g + 1) * gw] > 0.5
            m_prev = m_scr[g]
            m_new = jnp.where(sel, jnp.maximum(m_prev, jnp.max(s, axis=0, keepdims=True)), m_prev)
            alpha = jnp.exp2(m_prev - m_new)
            p = jnp.exp2(s - jnp.where(sel, m_new, jnp.inf))
            m_scr[g] = m_new
            acc_scr[g] = alpha * acc_scr[g] + _mm(v_group(j, g), p, False)

    def pair_of_blocks(jj, carry):
        past_blocks([2 * jj, 2 * jj + 1])
        return carry

    own_block()
    lax.fori_loop(0, i // 2, pair_of_blocks, 0)

    @pl.when(i % 2 == 1)
    def _():
        past_blocks([i - 1])

    tiles = []
    for g in range(N_KV_HEADS):
        acc = acc_scr[g]
        og = acc[:HEAD_DIM, :] / acc[HEAD_DIM:HEAD_DIM + 1, :]
        tiles += [og[:, r * blk:(r + 1) * blk] for r in range(Q_PER_KV)]
    o_ref[0] = jnp.concatenate(tiles, axis=0).T.astype(o_ref.dtype)


def _moba_prompt(qx, kt, ktb, vtb):
    B, nb, _, _ = qx.shape
    blk = MOBA_BLOCK
    S = nb * blk
    gw = Q_PER_KV * blk
    nb_pad = -(-nb // SUBLANES) * SUBLANES
    assert nb_pad <= LANES
    cdt = BF16
    v_rows = N_KV_HEADS * V_ROWS
    grid_spec = pl.GridSpec(
        grid=(B, nb),
        in_specs=[
            pl.BlockSpec((1, 1, HEAD_DIM, N_HEADS * blk), lambda b, i: (b, i, 0, 0)),
            pl.BlockSpec((1, KV_WIDTH, blk), lambda b, i: (b, 0, i)),
            pl.BlockSpec((1, nb, KV_WIDTH, blk), lambda b, i: (b, 0, 0, 0)),
            pl.BlockSpec((1, nb, v_rows, blk), lambda b, i: (b, 0, 0, 0)),
        ],
        out_specs=pl.BlockSpec((1, blk, ATT_WIDTH), lambda b, i: (b, i, 0)),
        scratch_shapes=[
            pltpu.VMEM((N_KV_HEADS, V_ROWS, gw), F32),
            pltpu.VMEM((N_KV_HEADS, 1, gw), F32),
            pltpu.VMEM((nb_pad, N_KV_HEADS * gw), F32),
            pltpu.VMEM((KV_WIDTH, LANES), F32),
        ],
    )
    blkb = (_nbytes((HEAD_DIM, N_HEADS * blk), cdt) + _nbytes((KV_WIDTH, blk), F32)
            + _nbytes((nb, KV_WIDTH, blk), cdt) + _nbytes((nb, v_rows, blk), cdt) + _nbytes((blk, ATT_WIDTH), cdt))
    scr = (_nbytes((N_KV_HEADS, V_ROWS, gw), F32) + _nbytes((N_KV_HEADS, SUBLANES, gw), F32)
           + _nbytes((nb_pad, N_KV_HEADS * gw), F32) + _nbytes((KV_WIDTH, LANES), F32))
    tmp = 6 * _nbytes((blk, gw), F32)
    return pl.pallas_call(
        functools.partial(_moba_prompt_kernel, nb_pad=nb_pad),
        grid_spec=grid_spec,
        out_shape=jax.ShapeDtypeStruct((B, S, ATT_WIDTH), cdt),
        compiler_params=pltpu.CompilerParams(
            dimension_semantics=("parallel", "arbitrary"),
            vmem_limit_bytes=_vmem_limit(blkb, scr, tmp)),
        name="moba_prompt",
    )(qx, kt, ktb, vtb)


PAGES_PER_BLOCK = MOBA_BLOCK // PAGE_SIZE
SAMPLE_BLOCKS_PER_STEP = 8


def _moba_sample_kernel(pt_ref, q_ref, *rest, n_blocks, n_steps, dec_seq):
    del pt_ref
    npg = SAMPLE_BLOCKS_PER_STEP * PAGES_PER_BLOCK
    k_pages = rest[:npg]
    v_pages = rest[npg:2 * npg]
    knew_ref, vnew_ref, o_ref, m_scr, l_scr, g_scr, o_scr = rest[2 * npg:]
    s_id = pl.program_id(1)
    qbd = q_ref[0]
    rows = qbd.shape[0]

    for kk in range(SAMPLE_BLOCKS_PER_STEP):
        n = s_id * SAMPLE_BLOCKS_PER_STEP + kk
        ktb = jnp.concatenate([k_pages[kk * PAGES_PER_BLOCK + p][...] for p in range(PAGES_PER_BLOCK)], axis=1)
        vtb = jnp.concatenate([v_pages[kk * PAGES_PER_BLOCK + p][...] for p in range(PAGES_PER_BLOCK)], axis=1)
        sc = _mm(qbd, ktb, False)
        g_scr[n] = jnp.mean(sc, axis=1, keepdims=True)
        m = jnp.max(sc, axis=1, keepdims=True)
        p = jnp.exp(sc - m)
        m_scr[n] = m
        l_scr[n] = jnp.sum(p, axis=1, keepdims=True)
        o_scr[n] = _mm_nt(p, vtb, False)

    @pl.when(s_id == n_steps - 1)
    def _():
        kn = knew_ref[0]
        vn = vnew_ref[0]
        qidx = lax.broadcasted_iota(jnp.int32, (rows, 1), 0) % dec_seq
        cols = []
        for tt in range(dec_seq):
            c = jnp.sum(qbd * kn[tt:tt + 1, :], axis=1, keepdims=True)
            cols.append(jnp.where(tt <= qidx, c, NEG_INF))
        m_own = cols[0]
        for c in cols[1:]:
            m_own = jnp.maximum(m_own, c)
        l_own = jnp.zeros((rows, 1), F32)
        o_own = jnp.zeros((rows, KV_WIDTH), F32)
        for tt in range(dec_seq):
            pt = jnp.exp(cols[tt] - m_own)
            l_own = l_own + pt
            o_own = o_own + pt * vn[tt:tt + 1, :]
        lane = lax.broadcasted_iota(jnp.int32, (rows, LANES), 1)
        gmat = jnp.zeros((rows, LANES), F32)
        mmat = jnp.zeros((rows, LANES), F32)
        lmat = jnp.zeros((rows, LANES), F32)
        for n in range(n_blocks):
            gmat = jnp.where(lane == n, g_scr[n], gmat)
            mmat = jnp.where(lane == n, m_scr[n], mmat)
            lmat = jnp.where(lane == n, l_scr[n], lmat)
        cnt = jnp.zeros((rows, LANES), jnp.int32)
        for n2 in range(n_blocks):
            col = g_scr[n2]
            beats = (col > gmat) | ((col == gmat) & (n2 < lane))
            cnt = cnt + jnp.where(beats, 1, 0)
        sel = (lane < n_blocks) & (cnt < MOBA_TOPK)
        m_all = jnp.maximum(m_own, jnp.max(jnp.where(sel, mmat, NEG_INF), axis=1, keepdims=True))
        wmat = jnp.where(sel, jnp.exp(mmat - m_all), 0.0)
        w_own = jnp.exp(m_own - m_all)
        denom = w_own * l_own + jnp.sum(wmat * lmat, axis=1, keepdims=True)
        out = w_own * o_own
        for n in range(n_blocks):
            out = out + wmat[:, n:n + 1] * o_scr[n]
        o_ref[0] = out / denom


def _moba_sample(qbd, cache_k, cache_v, page_table, k_new, v_new):
    Bd, rows, _ = qbd.shape
    dec_seq = k_new.shape[1]
    n_pages = page_table.shape[1]
    assert (n_pages * PAGE_SIZE) % MOBA_BLOCK == 0
    n_blocks = n_pages // PAGES_PER_BLOCK
    assert n_blocks % SAMPLE_BLOCKS_PER_STEP == 0 and n_blocks <= LANES
    n_steps = n_blocks // SAMPLE_BLOCKS_PER_STEP
    npg = SAMPLE_BLOCKS_PER_STEP * PAGES_PER_BLOCK

    def page_spec(p):
        return pl.BlockSpec((None, KV_WIDTH, PAGE_SIZE), lambda b, s, pt: (pt[b, s * npg + p], 0, 0))

    grid_spec = pltpu.PrefetchScalarGridSpec(
        num_scalar_prefetch=1,
        grid=(Bd, n_steps),
        in_specs=([pl.BlockSpec((1, rows, KV_WIDTH), lambda b, s, pt: (b, 0, 0))]
                  + [page_spec(p) for p in range(npg)] + [page_spec(p) for p in range(npg)]
                  + [pl.BlockSpec((1, dec_seq, KV_WIDTH), lambda b, s, pt: (b, 0, 0))] * 2),
        out_specs=pl.BlockSpec((1, rows, KV_WIDTH), lambda b, s, pt: (b, 0, 0)),
        scratch_shapes=[
            pltpu.VMEM((n_blocks, rows, 1), F32),
            pltpu.VMEM((n_blocks, rows, 1), F32),
            pltpu.VMEM((n_blocks, rows, 1), F32),
            pltpu.VMEM((n_blocks, rows, KV_WIDTH), F32),
        ],
    )
    blkb = (2 * _nbytes((rows, KV_WIDTH), F32) + 2 * npg * _nbytes((PAGE_SIZE, KV_WIDTH), F32)
            + 2 * _nbytes((SUBLANES, KV_WIDTH), F32))
    scr = 3 * _nbytes((n_blocks, rows, LANES), F32) + _nbytes((n_blocks, rows, KV_WIDTH), F32)
    tmp = 8 * _nbytes((MOBA_BLOCK, KV_WIDTH), F32)
    return pl.pallas_call(
        functools.partial(_moba_sample_kernel, n_blocks=n_blocks, n_steps=n_steps, dec_seq=dec_seq),
        grid_spec=grid_spec,
        out_shape=jax.ShapeDtypeStruct((Bd, rows, KV_WIDTH), F32),
        compiler_params=pltpu.CompilerParams(
            dimension_semantics=("parallel", "arbitrary"),
            vmem_limit_bytes=_vmem_limit(blkb, scr, tmp)),
        name="moba_sample",
    )(page_table, qbd, *([cache_k] * npg), *([cache_v] * npg), k_new, v_new)


def _ssd_kernel(*refs, precise, n_valid, n_chunks, has_state, n_ssm_heads, d_inner):
    tc = SSD_CHUNK
    refs = list(refs)
    z_ref, xs_ref, bc_ref, dt_ref = refs[:4]
    del refs[:4]
    if has_state:
        h0_ref, pre_ref = refs[:2]
        del refs[:2]
    (cw_ref, cb_ref, dtb_ref, a_ref, dsk_ref, nw_ref, o_ref, st_ref, cv_ref,
     tail_scr, h_scr, xt_scr, xd_scr, yt_scr) = refs[:14]
    if n_valid < tc:
        pad_scr, dtp_scr, zp_scr = refs[14:]
    c = pl.program_id(1)
    conv_ch = tail_scr.shape[1]
    gw = SSM_GROUPS * D_STATE
    heads_per_group = n_ssm_heads // SSM_GROUPS
    rows_per_group = heads_per_group * SSM_HEAD_DIM
    n_pre = CONV_WIDTH - 1

    @pl.when(c == 0)
    def _():
        tail_scr[...] = jnp.zeros_like(tail_scr)
        if has_state:
            tail_scr[SUBLANES - n_pre:SUBLANES, :] = pre_ref[0]
            h_scr[...] = h0_ref[0]
        else:
            h_scr[...] = jnp.zeros_like(h_scr)

    if n_valid == tc:
        raw = jnp.concatenate([xs_ref[0], bc_ref[0]], axis=1)
    else:
        pad_scr[...] = jnp.zeros_like(pad_scr)
        pad_scr[0:n_valid, :d_inner] = xs_ref[0]
        pad_scr[0:n_valid, d_inner:] = bc_ref[0]
        raw = pad_scr[...]
    tail = tail_scr[...]
    last = raw[tc - SUBLANES:tc, :]
    sub = lax.broadcasted_iota(jnp.int32, (SUBLANES, conv_ch), 0)
    conv = cb_ref[...] + cw_ref[CONV_WIDTH - 1:CONV_WIDTH, :] * raw
    for k in range(1, CONV_WIDTH):
        src = jnp.concatenate([raw[:tc - SUBLANES, :], jnp.where(sub >= SUBLANES - k, tail, last)], axis=0)
        conv = conv + cw_ref[CONV_WIDTH - 1 - k:CONV_WIDTH - k, :] * pltpu.roll(src, k, 0)
    xc = _silu(conv)
    tail_scr[...] = last

    @pl.when(c == n_chunks - 1)
    def _():
        cv_ref[0, :, :d_inner] = xs_ref[0, n_valid - n_pre:n_valid, :]
        cv_ref[0, :, d_inner:] = bc_ref[0, n_valid - n_pre:n_valid, :]

    if n_valid == tc:
        dt_raw = dt_ref[0]
    else:
        dtp_scr[...] = jnp.zeros_like(dtp_scr)
        dtp_scr[0:n_valid, :] = dt_ref[0]
        dt_raw = dtp_scr[...]
    xdt_in = dt_raw + dtb_ref[...]
    dt = jnp.maximum(xdt_in, 0.0) + jnp.log1p(jnp.exp(-jnp.abs(xdt_in)))
    t_iota = lax.broadcasted_iota(jnp.int32, (tc, LANES), 0)
    dt = jnp.where(t_iota < n_valid, dt, 0.0)
    d_a = dt * a_ref[...]
    tri = (lax.broadcasted_iota(jnp.int32, (tc, tc), 0) >= lax.broadcasted_iota(jnp.int32, (tc, tc), 1))
    acum = jnp.dot(tri.astype(F32), d_a, precision=lax.Precision.HIGHEST, preferred_element_type=F32)
    dt_t = dt.T
    acum_t = acum.T
    a_last = acum_t[:, tc - 1:tc]
    exp_a_t = jnp.exp(acum_t)
    dec_t = jnp.exp(a_last - acum_t)
    chunk_dec = jnp.broadcast_to(jnp.exp(a_last), (LANES, D_STATE))

    xt_scr[...] = xc[:, :d_inner].T
    s_iota = lax.broadcasted_iota(jnp.int32, (tc, tc), 0)
    u_iota = lax.broadcasted_iota(jnp.int32, (tc, tc), 1)
    causal = s_iota <= u_iota
    for g in range(SSM_GROUPS):
        bg = xc[:, d_inner + g * D_STATE:d_inner + (g + 1) * D_STATE]
        cg = xc[:, d_inner + gw + g * D_STATE:d_inner + gw + (g + 1) * D_STATE]
        cbt = _mm_nt(bg, cg, precise)
        rs = slice(g * rows_per_group, (g + 1) * rows_per_group)
        hg = h_scr[rs, :]
        yoff = _mm_nt(hg, cg, precise)
        for hh in range(heads_per_group):
            h = g * heads_per_group + hh
            hs = slice(h * SSM_HEAD_DIM, (h + 1) * SSM_HEAD_DIM)
            xh = xt_scr[hs, :]
            xdt = xh * dt_t[h:h + 1, :]
            seg = acum_t[h:h + 1, :] - acum[:, h:h + 1]
            lmat = jnp.exp(jnp.where(causal, seg, NEG_INF))
            yd = _mm(xdt, cbt * lmat, precise)
            yt_scr[hs, :] = (yd + yoff[hh * SSM_HEAD_DIM:(hh + 1) * SSM_HEAD_DIM, :] * exp_a_t[h:h + 1, :]
                             + dsk_ref[h:h + 1, :] * xh)
            xd_scr[hs, :] = xdt * dec_t[h:h + 1, :]
        st = _mm(xd_scr[rs, :], bg, precise)
        for hh in range(heads_per_group):
            h = g * heads_per_group + hh
            hs = slice(h * SSM_HEAD_DIM, (h + 1) * SSM_HEAD_DIM)
            lo = hh * SSM_HEAD_DIM
            h_scr[hs, :] = hg[lo:lo + SSM_HEAD_DIM, :] * chunk_dec[h:h + 1, :] + st[lo:lo + SSM_HEAD_DIM, :]

    @pl.when(c == n_chunks - 1)
    def _():
        st_ref[0] = h_scr[...]

    y = yt_scr[...].T
    if n_valid == tc:
        zz = z_ref[0]
    else:
        zp_scr[...] = jnp.zeros_like(zp_scr)
        zp_scr[0:n_valid, :] = z_ref[0]
        zz = zp_scr[...]
    yz = y * _silu(zz)
    gsz = d_inner // SSM_GROUPS
    outs = []
    for g in range(SSM_GROUPS):
        grp = yz[:, g * gsz:(g + 1) * gsz]
        outs.append(grp * lax.rsqrt(jnp.mean(grp * grp, axis=-1, keepdims=True) + RMS_EPS))
    res = jnp.concatenate(outs, axis=1) * nw_ref[...]
    o_ref[0] = res[:n_valid, :].astype(o_ref.dtype)


def _ssd(main3, dt3, h0, prefix, conv_w, conv_b, dt_bias, a_log, d_skip, ssm_norm_w, *, precise):
    B, S, _ = main3.shape
    d_inner = ssm_norm_w.shape[0]
    conv_ch = conv_w.shape[1]
    bc_w = conv_ch - d_inner
    assert (2 * d_inner) % bc_w == 0
    n_heads = d_inner // SSM_HEAD_DIM
    tc = SSD_CHUNK
    if S % tc == 0:
        n_valid, n_chunks = tc, S // tc
    else:
        assert S < tc
        n_valid, n_chunks = S, 1
    has_state = h0 is not None
    cdt = F32 if precise else BF16
    pad_h = LANES - n_heads
    dtb = jnp.pad(dt_bias.astype(F32), (0, pad_h)).reshape(1, LANES)
    a_row = jnp.pad(-jnp.exp(a_log.astype(F32)), (0, pad_h)).reshape(1, LANES)
    dsk = jnp.broadcast_to(jnp.pad(d_skip.astype(F32), (0, pad_h))[:, None], (LANES, tc))

    full2 = lambda shp: pl.BlockSpec(shp, lambda b, c: (0,) * len(shp))
    in_specs = [
        pl.BlockSpec((1, n_valid, d_inner), lambda b, c: (b, c, 0)),
        pl.BlockSpec((1, n_valid, d_inner), lambda b, c: (b, c, 1)),
        pl.BlockSpec((1, n_valid, bc_w), lambda b, c: (b, c, 2 * d_inner // bc_w)),
        pl.BlockSpec((1, n_valid, LANES), lambda b, c: (b, c, 0)),
    ]
    args = [main3, main3, main3, dt3]
    if has_state:
        in_specs += [pl.BlockSpec((1, d_inner, D_STATE), lambda b, c: (b, 0, 0)),
                     pl.BlockSpec((1, CONV_WIDTH - 1, conv_ch), lambda b, c: (b, 0, 0))]
        args += [h0, prefix]
    in_specs += [full2((CONV_WIDTH, conv_ch)), full2((1, conv_ch)), full2((1, LANES)), full2((1, LANES)),
                 full2((LANES, tc)), full2((1, d_inner))]
    args += [conv_w, conv_b.reshape(1, conv_ch), dtb, a_row, dsk, ssm_norm_w.reshape(1, d_inner)]
    out_shape = [jax.ShapeDtypeStruct((B, S, d_inner), cdt),
                 jax.ShapeDtypeStruct((B, d_inner, D_STATE), F32),
                 jax.ShapeDtypeStruct((B, CONV_WIDTH - 1, conv_ch), F32)]
    out_specs = [pl.BlockSpec((1, n_valid, d_inner), lambda b, c: (b, c, 0)),
                 pl.BlockSpec((1, d_inner, D_STATE), lambda b, c: (b, 0, 0)),
                 pl.BlockSpec((1, CONV_WIDTH - 1, conv_ch), lambda b, c: (b, 0, 0))]
    assert n_valid >= CONV_WIDTH - 1
    scratch = [pltpu.VMEM((SUBLANES, conv_ch), F32),
               pltpu.VMEM((d_inner, D_STATE), F32),
               pltpu.VMEM((d_inner, tc), F32), pltpu.VMEM((d_inner, tc), F32), pltpu.VMEM((d_inner, tc), F32)]
    scr = _nbytes((SUBLANES, conv_ch), F32) + 4 * _nbytes((d_inner, D_STATE), F32)
    if n_valid < tc:
        scratch += [pltpu.VMEM((tc, conv_ch), F32), pltpu.VMEM((tc, LANES), F32), pltpu.VMEM((tc, d_inner), F32)]
        scr += _nbytes((tc, conv_ch), F32) + _nbytes((tc, LANES), F32) + _nbytes((tc, d_inner), F32)
    blk = (_nbytes((tc, d_inner), F32) + _nbytes((tc, conv_ch), F32) + _nbytes((tc, LANES), F32)
           + 2 * _nbytes((d_inner, D_STATE), F32) + 2 * _nbytes((SUBLANES, conv_ch), F32)
           + _nbytes((CONV_WIDTH + SUBLANES, conv_ch), F32) + _nbytes((tc, d_inner), cdt))
    tmp = 3 * _nbytes((tc, conv_ch), F32) + 4 * _nbytes((tc, d_inner), F32)
    return pl.pallas_call(
        functools.partial(_ssd_kernel, precise=precise, n_valid=n_valid, n_chunks=n_chunks,
                          has_state=has_state, n_ssm_heads=n_heads, d_inner=d_inner),
        grid=(B, n_chunks),
        in_specs=in_specs,
        out_specs=out_specs,
        out_shape=out_shape,
        scratch_shapes=scratch,
        compiler_params=pltpu.CompilerParams(
            dimension_semantics=("parallel", "arbitrary"),
            vmem_limit_bytes=_vmem_limit(blk, scr, tmp)),
        name="ssd",
    )(*args)


def _merge_kernel(x_ref, att_ref, ssd_ref, ga_ref, gb_ref, wpa_ref, wpb_ref, wo_ref, o_ref, *, precise):
    a = _mm(att_ref[...], wpa_ref[...], precise)
    b = _mm(ssd_ref[...], wpb_ref[...], precise)
    m = _sigmoid(ga_ref[...]) * a + _sigmoid(gb_ref[...]) * b
    o_ref[...] = x_ref[...] + _mm(m, wo_ref[...], precise)


def _merge(x2d, att, ssd, main2, gate_col, w_pa, w_pb, w_o, *, precise):
    T, D = x2d.shape
    d_inner = ssd.shape[1]
    tm = min(512, T)
    assert T % tm == 0 and gate_col % D == 0
    cdt = F32 if precise else BF16
    row = lambda w: pl.BlockSpec((tm, w), lambda i: (i, 0))
    gate = lambda k: pl.BlockSpec((tm, D), lambda i: (i, gate_col // D + k))
    full = lambda r, w: pl.BlockSpec((r, w), lambda i: (0, 0))
    blk = (4 * _nbytes((tm, D), F32) + _nbytes((tm, ATT_WIDTH), cdt) + _nbytes((tm, d_inner), cdt)
           + _nbytes((ATT_WIDTH + d_inner + D, D), cdt))
    tmp = 4 * _nbytes((tm, D), F32)
    return pl.pallas_call(
        functools.partial(_merge_kernel, precise=precise),
        grid=(T // tm,),
        in_specs=[row(D), row(ATT_WIDTH), row(d_inner), gate(0), gate(1),
                  full(ATT_WIDTH, D), full(d_inner, D), full(D, D)],
        out_specs=row(D),
        out_shape=jax.ShapeDtypeStruct((T, D), F32),
        compiler_params=pltpu.CompilerParams(
            dimension_semantics=("parallel",),
            vmem_limit_bytes=_vmem_limit(blk, 0, tmp)),
        name="merge",
    )(x2d, att, ssd, main2, main2, w_pa, w_pb, w_o)


def _moe_kernel(x_ref, n2_ref, wr_ref, wg_ref, wu_ref, wd_ref, nf_ref, o_ref, h_scr, gate_scr, acc_scr,
                *, precise, final_norm):
    e = pl.program_id(1)
    tm = x_ref.shape[0]
    epg = EXPERTS_PER_GROUP

    @pl.when(e == 0)
    def _():
        x = x_ref[...]
        h = x * lax.rsqrt(jnp.mean(x * x, axis=-1, keepdims=True) + RMS_EPS) * n2_ref[...]
        h_scr[...] = h.astype(h_scr.dtype)
        acc_scr[...] = jnp.zeros_like(acc_scr)
        lg = _mm_nt(wr_ref[...], h, precise)
        gl = [lg[k:k + 1, :] for k in range(N_EXPERT_GROUPS)]
        gmax = functools.reduce(jnp.maximum, gl)
        taken = jnp.zeros((1, tm), jnp.bool_)
        is_g = []
        for k in range(N_EXPERT_GROUPS):
            hit = (gl[k] == gmax) & (~taken)
            is_g.append(hit)
            taken = taken | hit
        p_grp = 1.0 / functools.reduce(lambda a, b: a + b, [jnp.exp(v - gmax) for v in gl])
        le = []
        for k in range(epg):
            v = jnp.zeros((1, tm), F32)
            for gi in range(N_EXPERT_GROUPS):
                r0 = N_EXPERT_GROUPS + gi * epg + k
                v = jnp.where(is_g[gi], lg[r0:r0 + 1, :], v)
            le.append(v)

        def first_max(vals):
            mx = functools.reduce(jnp.maximum, vals)
            tk = jnp.zeros((1, tm), jnp.bool_)
            hits = []
            for v in vals:
                hit = (v == mx) & (~tk)
                hits.append(hit)
                tk = tk | hit
            return mx, hits

        v1, o1 = first_max(le)
        v2, o2 = first_max([jnp.where(o1[k], NEG_INF, le[k]) for k in range(epg)])
        e21 = jnp.exp(v2 - v1)
        w1 = p_grp / (1.0 + e21)
        w2 = p_grp * e21 / (1.0 + e21)
        wk = [jnp.where(o1[k], w1, jnp.where(o2[k], w2, 0.0)) for k in range(epg)]
        row = lax.broadcasted_iota(jnp.int32, (LANES, tm), 0)
        for gi in range(N_EXPERT_GROUPS):
            gm = jnp.zeros((LANES, tm), F32)
            for k in range(epg):
                gm = jnp.where(row == k, jnp.where(is_g[gi], wk[k], 0.0), gm)
            gate_scr[gi] = gm.T

    h = h_scr[...]
    gate = gate_scr[e]
    cdt = F32 if precise else BF16
    acc = acc_scr[...]
    for k in range(epg):
        hg = _mm(h, wg_ref[k], precise)
        hu = _mm(h, wu_ref[k], precise)
        hid = (_silu(hg) * hu * gate[:, k:k + 1]).astype(cdt)
        acc = acc + _mm(hid, wd_ref[k], precise)
    acc_scr[...] = acc

    @pl.when(e == N_EXPERT_GROUPS - 1)
    def _():
        y = x_ref[...] + acc_scr[...]
        if final_norm:
            y = y * lax.rsqrt(jnp.mean(y * y, axis=-1, keepdims=True) + RMS_EPS) * nf_ref[...]
        o_ref[...] = y


def _moe(x2d, norm2_w, w_router, w_g, w_u, w_d, norm_f_w, *, precise, final_norm):
    T, D = x2d.shape
    n_e, _, d_e = w_g.shape
    assert n_e == N_EXPERTS
    tm = min(1024, T)
    assert T % tm == 0
    cdt = F32 if precise else BF16
    epg = EXPERTS_PER_GROUP
    blk = (2 * _nbytes((tm, D), F32) + 2 * _nbytes((1, D), F32) + _nbytes((D, LANES), F32)
           + 3 * _nbytes((epg, D, d_e), cdt))
    scr = _nbytes((tm, D), cdt) + _nbytes((N_EXPERT_GROUPS, tm, LANES), F32) + _nbytes((tm, D), F32)
    tmp = 6 * _nbytes((tm, d_e), F32) + 2 * _nbytes((tm, D), F32) + 64 * _nbytes((tm, LANES), F32)
    return pl.pallas_call(
        functools.partial(_moe_kernel, precise=precise, final_norm=final_norm),
        grid=(T // tm, N_EXPERT_GROUPS),
        in_specs=[
            pl.BlockSpec((tm, D), lambda i, e: (i, 0)),
            pl.BlockSpec((1, D), lambda i, e: (0, 0)),
            pl.BlockSpec((LANES, D), lambda i, e: (0, 0)),
            pl.BlockSpec((epg, D, d_e), lambda i, e: (e, 0, 0)),
            pl.BlockSpec((epg, D, d_e), lambda i, e: (e, 0, 0)),
            pl.BlockSpec((epg, d_e, D), lambda i, e: (e, 0, 0)),
            pl.BlockSpec((1, D), lambda i, e: (0, 0)),
        ],
        out_specs=pl.BlockSpec((tm, D), lambda i, e: (i, 0)),
        out_shape=jax.ShapeDtypeStruct((T, D), F32),
        scratch_shapes=[pltpu.VMEM((tm, D), cdt), pltpu.VMEM((N_EXPERT_GROUPS, tm, LANES), F32),
                        pltpu.VMEM((tm, D), F32)],
        compiler_params=pltpu.CompilerParams(
            dimension_semantics=("parallel", "arbitrary"),
            vmem_limit_bytes=_vmem_limit(blk, scr, tmp)),
        name="moe",
    )(x2d, norm2_w.reshape(1, D), w_router, w_g, w_u, w_d, norm_f_w.reshape(1, D))


def _rope_tables(pos):
    inv_freq = 1.0 / (ROPE_THETA ** (jnp.arange(0, HEAD_DIM, 2, dtype=F32) / HEAD_DIM))
    ang = pos.astype(F32)[:, None] * inv_freq[None, :]
    ang = jnp.concatenate([ang, ang], axis=-1)
    sign = jnp.concatenate([-jnp.ones((HEAD_DIM // 2,), F32), jnp.ones((HEAD_DIM // 2,), F32)])
    reps = LANES // HEAD_DIM
    return jnp.tile(jnp.cos(ang), (1, reps)), jnp.tile(jnp.sin(ang) * sign, (1, reps))


def _layer_weights(lw, precise):
    (norm1_w, w_in, conv_w, conv_b, dt_bias, a_log, d_skip, ssm_norm_w,
     w_pa, w_pb, w_o, norm2_w, w_rg, w_re, w_g, w_u, w_d) = lw
    cdt = F32 if precise else BF16
    D = w_in.shape[0]
    d_inner = ssm_norm_w.shape[0]
    conv_ch = conv_w.shape[1]
    n_heads = dt_bias.shape[0]
    qkv_w = ATT_WIDTH + 2 * KV_WIDTH
    dt0 = qkv_w + d_inner + conv_ch
    w_qkv = w_in[:, :qkv_w].astype(cdt)
    w_rest = jnp.concatenate([w_in[:, qkv_w:dt0], w_in[:, dt0 + n_heads:]], axis=1).astype(cdt)
    w_dt = jnp.pad(w_in[:, dt0:dt0 + n_heads], ((0, 0), (0, LANES - n_heads))).astype(cdt)
    w_router = jnp.pad(jnp.concatenate([w_rg, w_re], axis=1).T,
                       ((0, LANES - N_EXPERT_GROUPS - N_EXPERTS), (0, 0))).astype(F32)
    return dict(norm1_w=norm1_w, w_qkv=w_qkv, w_rest=w_rest, w_dt=w_dt, conv_w=conv_w, conv_b=conv_b,
                dt_bias=dt_bias,
                a_log=a_log, d_skip=d_skip, ssm_norm_w=ssm_norm_w, w_pa=w_pa.astype(cdt),
                w_pb=w_pb.astype(cdt), w_o=w_o.astype(cdt), norm2_w=norm2_w, w_router=w_router,
                w_g=w_g.astype(cdt), w_u=w_u.astype(cdt), w_d=w_d.astype(cdt),
                d_inner=d_inner, conv_ch=conv_ch)


def _pages_transposed(cache):
    n_phys = cache.shape[0]
    return cache.transpose(0, 2, 3, 1).reshape(n_phys, KV_WIDTH, PAGE_SIZE)


def _prompt_layer(x3, W, norm_f_w, final_norm):
    B, S, D = x3.shape
    T = B * S
    cos_t, sin_t = _rope_tables(jnp.arange(S, dtype=jnp.int32))
    x2d = x3.reshape(T, D)
    qx, kt, vt, dt, ktb, vtb = _inproj_qkv(
        x2d, B, W["norm1_w"], W["w_qkv"], W["w_dt"], cos_t, sin_t,
        precise=False, q_scale=ATT_SCALE * LOG2E, q_blocks=True)
    main = _inproj_main(x2d, W["norm1_w"], W["w_rest"], precise=False)
    att = _moba_prompt(qx, kt, ktb, vtb)
    k = kt.reshape(B, N_KV_HEADS, HEAD_DIM, S).transpose(0, 3, 1, 2)
    v = vt.reshape(B, N_KV_HEADS, HEAD_DIM, S).transpose(0, 3, 1, 2)
    ssd, state, conv_new = _ssd(
        main.reshape(B, S, -1), dt.reshape(B, S, LANES), None, None,
        W["conv_w"], W["conv_b"], W["dt_bias"], W["a_log"], W["d_skip"], W["ssm_norm_w"], precise=False)
    x1 = _merge(x2d, att.reshape(T, ATT_WIDTH), ssd.reshape(T, -1), main, W["d_inner"] + W["conv_ch"],
                W["w_pa"], W["w_pb"], W["w_o"], precise=False)
    y = _moe(x1, W["norm2_w"], W["w_router"], W["w_g"], W["w_u"], W["w_d"], norm_f_w,
             precise=False, final_norm=final_norm)
    return y.reshape(B, S, D), k, v, state, conv_new


def _sample_layer(x3, W, norm_f_w, final_norm, cache_k, cache_v, page_table, h0, prefix):
    Bd, Sd, D = x3.shape
    T = Bd * Sd
    past_len = page_table.shape[1] * PAGE_SIZE
    pos = past_len + (jnp.arange(T, dtype=jnp.int32) % Sd)
    cos_t, sin_t = _rope_tables(pos)
    x2d = x3.reshape(T, D)
    q, kt, vt, dt = _inproj_qkv(
        x2d, 1, W["norm1_w"], W["w_qkv"], W["w_dt"], cos_t, sin_t,
        precise=True, q_scale=ATT_SCALE, q_blocks=False)
    main = _inproj_main(x2d, W["norm1_w"], W["w_rest"], precise=True)
    k = kt[0].T
    v = vt[0].T
    q = q.reshape(Bd, Sd, N_HEADS, HEAD_DIM).transpose(0, 2, 1, 3)
    grp_of_head = jnp.arange(N_HEADS) // Q_PER_KV
    onehot = (grp_of_head[:, None] == jnp.arange(N_KV_HEADS)[None, :]).astype(F32)
    qbd = (q[:, :, :, None, :] * onehot[None, :, None, :, None]).reshape(Bd, N_HEADS * Sd, KV_WIDTH)
    o = _moba_sample(qbd, cache_k, cache_v, page_table,
                     k.reshape(Bd, Sd, KV_WIDTH), v.reshape(Bd, Sd, KV_WIDTH))
    o = o.reshape(Bd, N_HEADS, Sd, N_KV_HEADS, HEAD_DIM)
    o = jnp.sum(o * onehot[None, :, None, :, None], axis=3)
    att = o.transpose(0, 2, 1, 3).reshape(T, ATT_WIDTH)
    ssd, state, conv_new = _ssd(
        main.reshape(Bd, Sd, -1), dt.reshape(Bd, Sd, LANES), h0, prefix,
        W["conv_w"], W["conv_b"], W["dt_bias"], W["a_log"], W["d_skip"], W["ssm_norm_w"], precise=True)
    x1 = _merge(x2d, att, ssd.reshape(T, -1), main, W["d_inner"] + W["conv_ch"],
                W["w_pa"], W["w_pb"], W["w_o"], precise=True)
    y = _moe(x1, W["norm2_w"], W["w_router"], W["w_g"], W["w_u"], W["w_d"], norm_f_w,
             precise=True, final_norm=final_norm)
    return y.reshape(Bd, Sd, D), k, v, state, conv_new


def kernel(x_prompt, x_sample, cache_k, cache_v, state_ssm, state_conv, page_table, norm1_w, w_in, conv_w,
           conv_b, dt_bias, a_log, d_skip, ssm_norm_w, w_pa, w_pb, w_o, norm2_w, w_router_group,
           w_router_expert, w_gate_e, w_up_e, w_down_e, norm_f_w):
    depth = w_in.shape[0]
    B, S, D = x_prompt.shape
    Bd, Sd, _ = x_sample.shape
    n_phys = cache_k.shape[1]
    hp, hs = x_prompt, x_sample
    outs = [[] for _ in range(8)]
    for l in range(depth):
        lw = (norm1_w[l], w_in[l], conv_w[l], conv_b[l], dt_bias[l], a_log[l], d_skip[l], ssm_norm_w[l],
              w_pa[l], w_pb[l], w_o[l], norm2_w[l], w_router_group[l], w_router_expert[l],
              w_gate_e[l], w_up_e[l], w_down_e[l])
        last = l == depth - 1
        hp, kp, vp, sp, cp = _prompt_layer(hp, _layer_weights(lw, False), norm_f_w, last)
        n_ssm_heads, ssm_hd, d_state = state_ssm.shape[2:]
        hs, ks, vs, ss, cs = _sample_layer(
            hs, _layer_weights(lw, True), norm_f_w, last,
            _pages_transposed(cache_k[l]), _pages_transposed(cache_v[l]),
            page_table, state_ssm[l].reshape(Bd, n_ssm_heads * ssm_hd, d_state), state_conv[l])
        outs[0].append(kp)
        outs[1].append(vp)
        outs[2].append(sp.reshape(B, n_ssm_heads, ssm_hd, d_state))
        outs[3].append(cp)
        outs[4].append(ks.reshape(Bd, Sd, N_KV_HEADS, HEAD_DIM))
        outs[5].append(vs.reshape(Bd, Sd, N_KV_HEADS, HEAD_DIM))
        outs[6].append(ss.reshape(Bd, n_ssm_heads, ssm_hd, d_state))
        outs[7].append(cs)
    return (hp, hs) + tuple(jnp.stack(o) for o in outs)
```
